```python
import math
import jax, jax.numpy as jnp
from jax import lax
import numpy as np

D_MODEL = 1024
BATCH = 8
SEQ = 8192
DEPTH = 2

CHUNK = 64
RMS_EPS = 1e-6
N_BRANCHES = 3
GLA_HEADS = 4
GLA_DK = 128
GLA_DV = 128
GLA_GATE_RANK = 16
GLA_GATE_NORM = 16.0
GLA_KEY_W = GLA_HEADS * GLA_DK
GLA_VAL_W = GLA_HEADS * GLA_DV
ATT_HEADS = 8
ATT_HD = 64
ATT_W = ATT_HEADS * ATT_HD
ATT_LEFT_CHUNKS = 8
REL_MAX = 256
REL_SIZE = REL_MAX + CHUNK
GDN_HEADS = 4
GDN_DK = 128
GDN_DV = 128
GDN_KEY_W = GDN_HEADS * GDN_DK
GDN_VAL_W = GDN_HEADS * GDN_DV
GDN_CONV_CH = 2 * GDN_KEY_W + GDN_VAL_W
CONV_W = 4
FFN_HIDDEN = -(-(8 * D_MODEL) // (3 * 256)) * 256
IN_WIDTHS = (GLA_KEY_W, GLA_KEY_W, GLA_VAL_W, GLA_VAL_W, GLA_GATE_RANK,
             ATT_W, ATT_W, ATT_W,
             GDN_KEY_W, GDN_KEY_W, GDN_VAL_W, GDN_VAL_W, GDN_HEADS, GDN_HEADS,
             N_BRANCHES * D_MODEL)
IN_W = sum(IN_WIDTHS)

kernel_name = 'hybrid_gla_chunkattn_gdn_block'

F32 = jnp.float32


def rmsnorm(x, g):
    xf = x.astype(F32)
    y = xf * lax.rsqrt(jnp.mean(xf * xf, axis=-1, keepdims=True) + RMS_EPS)
    return (y * g.astype(F32)).astype(x.dtype)


def gated_head_rmsnorm(o, gate, g, n_heads):
    b, s, w = o.shape
    oh = o.reshape(b, s, n_heads, w // n_heads).astype(F32)
    oh = oh * lax.rsqrt(jnp.mean(oh * oh, axis=-1, keepdims=True) + RMS_EPS) * g.astype(F32)
    return oh.reshape(b, s, w) * jax.nn.silu(gate.astype(F32))


def l2norm(t):
    return t * lax.rsqrt(jnp.sum(t * t, axis=-1, keepdims=True) + RMS_EPS)


def to_chunks(t, n_heads):
    b, s, w = t.shape
    return t.reshape(b, s // CHUNK, CHUNK, n_heads, w // n_heads).transpose(0, 3, 1, 2, 4)


def from_chunks(t):
    b, h, n, c, d = t.shape
    return t.transpose(0, 2, 3, 1, 4).reshape(b, n * c, h * d)


def causal_depthwise_conv(x, w):
    return lax.conv_general_dilated(
        x, w[:, None, :].astype(x.dtype), window_strides=(1,), padding=[(CONV_W - 1, 0)],
        dimension_numbers=('NWC', 'WIO', 'NWC'), feature_group_count=x.shape[-1])


def gla_mixer(q, k, v, r, gate_lr, w_gate_up, b_gate, norm_g):
    b, s, _ = q.shape
    log_a = jax.nn.log_sigmoid((gate_lr @ w_gate_up + b_gate).astype(F32)) / GLA_GATE_NORM
    qc = to_chunks(q.astype(F32) * GLA_DK ** -0.5, GLA_HEADS)
    kc = to_chunks(k.astype(F32), GLA_HEADS)
    vc = to_chunks(v.astype(F32), GLA_HEADS)
    gc = jnp.cumsum(to_chunks(log_a, GLA_HEADS), axis=3)
    causal = jnp.tril(jnp.ones((CHUNK, CHUNK), dtype=bool))

    def step(state, inp):
        q_, k_, v_, g_ = inp
        diff = jnp.where(causal[:, :, None], g_[:, :, :, None, :] - g_[:, :, None, :, :], -jnp.inf)
        scores = jnp.einsum('bhik,bhijk->bhij', q_, jnp.exp(diff) * k_[:, :, None, :, :])
        g_last = g_[:, :, -1, :]
        o = (jnp.einsum('bhik,bhkv->bhiv', q_ * jnp.exp(g_), state)
             + jnp.einsum('bhij,bhjv->bhiv', scores, v_))
        state = (jnp.exp(g_last)[..., None] * state
                 + jnp.einsum('bhjk,bhjv->bhkv', k_ * jnp.exp(g_last[:, :, None, :] - g_), v_))
        return state, o

    init = jnp.zeros((b, GLA_HEADS, GLA_DK, GLA_DV), F32)
    xs = (jnp.moveaxis(qc, 2, 0), jnp.moveaxis(kc, 2, 0), jnp.moveaxis(vc, 2, 0), jnp.moveaxis(gc, 2, 0))
    _, o = lax.scan(step, init, xs)
    o = from_chunks(jnp.moveaxis(o, 0, 2))
    return gated_head_rmsnorm(o, r, norm_g, GLA_HEADS).astype(q.dtype)


def chunk_band_attention(q, k, v, rel_bias):
    b, s, _ = q.shape
    n = s // CHUNK
    left = ATT_LEFT_CHUNKS * CHUNK
    band = left + CHUNK

    def heads(t):
        return t.reshape(b, s, ATT_HEADS, ATT_HD).transpose(0, 2, 1, 3)

    qh = heads(q) * ATT_HD ** -0.5
    kh = jnp.pad(heads(k), ((0, 0), (0, 0), (left, 0), (0, 0)))
    vh = jnp.pad(heads(v), ((0, 0), (0, 0), (left, 0), (0, 0)))
    rel = jnp.arange(CHUNK)[:, None] + left - jnp.arange(band)[None, :]
    bias = rel_bias[:, jnp.clip(rel, -(CHUNK - 1), REL_MAX) + (CHUNK - 1)].astype(F32)

    def one_chunk(c):
        start = c * CHUNK
        q_ = lax.dynamic_slice_in_dim(qh, start, CHUNK, axis=2)
        k_ = lax.dynamic_slice_in_dim(kh, start, band, axis=2)
        v_ = lax.dynamic_slice_in_dim(vh, start, band, axis=2)
        scores = jnp.einsum('bhid,bhjd->bhij', q_, k_, preferred_element_type=F32) + bias
        valid = (start - left + jnp.arange(band)) >= 0
        p = jax.nn.softmax(jnp.where(valid, scores, -jnp.inf), axis=-1)
        return jnp.einsum('bhij,bhjd->bhid', p.astype(v_.dtype), v_)

    o = lax.map(one_chunk, jnp.arange(n))
    return o.transpose(1, 0, 3, 2, 4).reshape(b, s, ATT_W)


def gated_deltanet(q, k, v, z, a, b_logit, conv_w, a_log, dt_bias, norm_g):
    out_dtype = q.dtype
    qkv = causal_depthwise_conv(jnp.concatenate([q, k, v], axis=-1), conv_w)
    qkv = jax.nn.silu(qkv.astype(F32))
    q, k, v = jnp.split(qkv, [GDN_KEY_W, 2 * GDN_KEY_W], axis=-1)
    log_alpha = -jnp.exp(a_log.astype(F32)) * jax.nn.softplus(a.astype(F32) + dt_bias.astype(F32))
    beta = jax.nn.sigmoid(b_logit.astype(F32))
    qc = l2norm(to_chunks(q, GDN_HEADS)) * GDN_DK ** -0.5
    kc = l2norm(to_chunks(k, GDN_HEADS))
    vc = to_chunks(v, GDN_HEADS)
    gc = jnp.cumsum(to_chunks(log_alpha, GDN_HEADS)[..., 0], axis=-1)
    bc = to_chunks(beta, GDN_HEADS)
    incl = jnp.tril(jnp.ones((CHUNK, CHUNK), dtype=bool))
    strict = jnp.tril(jnp.ones((CHUNK, CHUNK), dtype=bool), k=-1)
    gamma = jnp.exp(jnp.where(incl, gc[..., :, None] - gc[..., None, :], -jnp.inf))
    k_beta = kc * bc
    a_mat = jnp.where(strict, jnp.einsum('bhnid,bhnjd->bhnij', k_beta, kc) * gamma, 0.0)
    rhs = jnp.concatenate([vc * bc, k_beta * jnp.exp(gc)[..., None]], axis=-1)
    sol = lax.linalg.triangular_solve(a_mat + jnp.eye(CHUNK, dtype=F32), rhs,
                                      left_side=True, lower=True, unit_diagonal=True)
    u = sol[..., :GDN_DV]
    w = sol[..., GDN_DV:]
    attn_qk = jnp.where(incl, jnp.einsum('bhnid,bhnjd->bhnij', qc, kc) * gamma, 0.0)

    def step(state, inp):
        q_, k_, u_, w_, g_, a_ = inp
        v_new = u_ - jnp.einsum('bhik,bhkv->bhiv', w_, state)
        o = (jnp.einsum('bhik,bhkv->bhiv', q_ * jnp.exp(g_)[..., None], state)
             + jnp.einsum('bhij,bhjv->bhiv', a_, v_new))
        g_last = g_[..., -1]
        state = (state * jnp.exp(g_last)[..., None, None]
                 + jnp.einsum('bhjk,bhjv->bhkv', k_ * jnp.exp(g_last[..., None] - g_)[..., None], v_new))
        return state, o

    bsz = qc.shape[0]
    init = jnp.zeros((bsz, GDN_HEADS, GDN_DK, GDN_DV), F32)
    xs = tuple(jnp.moveaxis(t, 2, 0) for t in (qc, kc, u, w, gc, attn_qk))
    _, o = lax.scan(step, init, xs)
    o = from_chunks(jnp.moveaxis(o, 0, 2))
    return gated_head_rmsnorm(o, z, norm_g, GDN_HEADS).astype(out_dtype)


def split_columns(t, widths):
    bounds, acc = [], 0
    for wdt in widths[:-1]:
        acc += wdt
        bounds.append(acc)
    return jnp.split(t, bounds, axis=-1)


def hybrid_layer(x, mix_norm_g, w_in, gla_w_gate_up, gla_b_gate, gla_norm_g, att_rel_bias,
                 gdn_conv_w, gdn_a_log, gdn_dt_bias, gdn_norm_g, w_branch_gla, w_branch_att,
                 w_branch_gdn, w_out, ffn_norm_g, w_ffn_in, w_ffn_out):
    b, s, _ = x.shape
    h = rmsnorm(x, mix_norm_g)
    proj = h @ w_in
    (gla_q, gla_k, gla_v, gla_r, gla_lr, att_q, att_k, att_v,
     gdn_q, gdn_k, gdn_v, gdn_z, gdn_a, gdn_b, merge) = split_columns(proj, IN_WIDTHS)
    o_gla = gla_mixer(gla_q, gla_k, gla_v, gla_r, gla_lr, gla_w_gate_up, gla_b_gate, gla_norm_g)
    o_att = chunk_band_attention(att_q, att_k, att_v, att_rel_bias)
    o_gdn = gated_deltanet(gdn_q, gdn_k, gdn_v, gdn_z, gdn_a, gdn_b, gdn_conv_w, gdn_a_log,
                           gdn_dt_bias, gdn_norm_g)
    gates = jax.nn.sigmoid(merge.astype(F32)).reshape(b, s, N_BRANCHES, D_MODEL)
    y = (gates[:, :, 0] * (o_gla @ w_branch_gla).astype(F32)
         + gates[:, :, 1] * (o_att @ w_branch_att).astype(F32)
         + gates[:, :, 2] * (o_gdn @ w_branch_gdn).astype(F32))
    x = x + y.astype(x.dtype) @ w_out
    h2 = rmsnorm(x, ffn_norm_g)
    gate, up = jnp.split(h2 @ w_ffn_in, 2, axis=-1)
    return x + (jax.nn.silu(gate) * up) @ w_ffn_out


def setup_inputs(seed: int = 0) -> dict:
    key = jax.random.key(seed)
    ks = jax.random.split(key, 20)

    def nrm(k, shape, fan_in):
        return jax.random.normal(k, shape, F32) * fan_in ** -0.5

    def gain(k, shape):
        return 1.0 + 0.01 * jax.random.normal(k, shape, F32)

    dt = jnp.exp(jax.random.uniform(ks[9], (DEPTH, GDN_HEADS), F32)
                 * (math.log(0.1) - math.log(0.001)) + math.log(0.001))
    return {
        'x': jax.random.normal(ks[0], (BATCH, SEQ, D_MODEL), F32),
        'mix_norm_g': gain(ks[1], (DEPTH, D_MODEL)),
        'w_in': nrm(ks[2], (DEPTH, D_MODEL, IN_W), D_MODEL),
        'gla_w_gate_up': nrm(ks[3], (DEPTH, GLA_GATE_RANK, GLA_KEY_W), GLA_GATE_RANK),
        'gla_b_gate': 0.1 * jax.random.normal(ks[4], (DEPTH, GLA_KEY_W), F32),
        'gla_norm_g': gain(ks[5], (DEPTH, GLA_DV)),
        'att_rel_bias': 0.1 * jax.random.normal(ks[6], (DEPTH, ATT_HEADS, REL_SIZE), F32),
        'gdn_conv_w': nrm(ks[7], (DEPTH, CONV_W, GDN_CONV_CH), CONV_W),
        'gdn_a_log': jnp.log(jax.random.uniform(ks[8], (DEPTH, GDN_HEADS), F32, 1.0, 16.0)),
        'gdn_dt_bias': dt + jnp.log(-jnp.expm1(-dt)),
        'gdn_norm_g': gain(ks[10], (DEPTH, GDN_DV)),
        'w_branch_gla': nrm(ks[11], (DEPTH, GLA_VAL_W, D_MODEL), GLA_VAL_W),
        'w_branch_att': nrm(ks[12], (DEPTH, ATT_W, D_MODEL), ATT_W),
        'w_branch_gdn': nrm(ks[13], (DEPTH, GDN_VAL_W, D_MODEL), GDN_VAL_W),
        'w_out': nrm(ks[14], (DEPTH, D_MODEL, D_MODEL), D_MODEL),
        'ffn_norm_g': gain(ks[15], (DEPTH, D_MODEL)),
        'w_ffn_in': nrm(ks[16], (DEPTH, D_MODEL, 2 * FFN_HIDDEN), D_MODEL),
        'w_ffn_out': nrm(ks[17], (DEPTH, FFN_HIDDEN, D_MODEL), FFN_HIDDEN),
        'final_norm_g': gain(ks[18], (D_MODEL,)),
    }


def reference(x, mix_norm_g, w_in, gla_w_gate_up, gla_b_gate, gla_norm_g, att_rel_bias,
              gdn_conv_w, gdn_a_log, gdn_dt_bias, gdn_norm_g, w_branch_gla, w_branch_att,
              w_branch_gdn, w_out, ffn_norm_g, w_ffn_in, w_ffn_out, final_norm_g):
    for l in range(DEPTH):
        x = hybrid_layer(x, mix_norm_g[l], w_in[l], gla_w_gate_up[l], gla_b_gate[l], gla_norm_g[l],
                         att_rel_bias[l], gdn_conv_w[l], gdn_a_log[l], gdn_dt_bias[l], gdn_norm_g[l],
                         w_branch_gla[l], w_branch_att[l], w_branch_gdn[l], w_out[l], ffn_norm_g[l],
                         w_ffn_in[l], w_ffn_out[l])
    return rmsnorm(x, final_norm_g)
```

```python
import functools

import numpy as np
import jax
import jax.numpy as jnp
from jax import lax
from jax.experimental import pallas as pl
from jax.experimental.pallas import tpu as pltpu

F32 = jnp.float32
BF16 = jnp.bfloat16

CHUNK = 64
RMS_EPS = 1e-6
GLA_HEADS = 4
GLA_DK = 128
GLA_GATE_RANK = 16
GLA_GATE_NORM = 16.0
ATT_HEADS = 8
ATT_HD = 64
ATT_LEFT_CHUNKS = 8
REL_MAX = 256
GDN_HEADS = 4
GDN_DK = 128
CONV_W = 4
HEAD_W = 512
LANES = 128
SMALL_A = GLA_GATE_RANK
SMALL_B = GLA_GATE_RANK + GDN_HEADS

CB_GLA_Q, CB_GLA_K, CB_GLA_V, CB_GLA_R = 6, 7, 8, 9
CB_ATT_Q, CB_ATT_K, CB_ATT_V = 10, 11, 12
CB_GDN_Q, CB_GDN_K, CB_GDN_V, CB_GDN_Z = 13, 14, 15, 16
MAIN_W = 17 * HEAD_W

VMEM_LIMIT = 56 * 1024 * 1024


def _cparams(sem):
    return pltpu.CompilerParams(dimension_semantics=sem, vmem_limit_bytes=VMEM_LIMIT)


def _dot(a, b):
    return jnp.dot(a.astype(BF16), b.astype(BF16), preferred_element_type=F32)


def _dot_nt(a, b):
    return lax.dot_general(a.astype(BF16), b.astype(BF16), (((1,), (1,)), ((), ())),
                           preferred_element_type=F32)


def _dot_tn(a, b):
    return lax.dot_general(a.astype(BF16), b.astype(BF16), (((0,), (0,)), ((), ())),
                           preferred_element_type=F32)


def _split(x):
    hi = x.astype(BF16)
    lo = (x - hi.astype(F32)).astype(BF16)
    return hi, lo


def _dot_sel(m_bf16, x):
    hi, lo = _split(x)
    return (jnp.dot(m_bf16, hi, preferred_element_type=F32)
            + jnp.dot(m_bf16, lo, preferred_element_type=F32))


def _dot3(a, b):
    ah, al = _split(a)
    bh, bl = _split(b)
    return (jnp.dot(ah, bh, preferred_element_type=F32)
            + jnp.dot(al, bh, preferred_element_type=F32)
            + jnp.dot(ah, bl, preferred_element_type=F32))


def _sigmoid(x):
    return 1.0 / (1.0 + jnp.exp(-x))


def _silu(x):
    return x * _sigmoid(x)


def _softplus(x):
    return jnp.maximum(x, 0.0) + jnp.log1p(jnp.exp(-jnp.abs(x)))


def _level_sizes():
    s, out = CHUNK // 2, []
    while s >= 1:
        out.append(s)
        s //= 2
    return out


def _np_masks():
    i = np.arange(CHUNK)[:, None]
    j = np.arange(CHUNK)[None, :]
    ms = [(j <= i), (j < i)]
    for s in _level_sizes():
        bi, bj = i // s, j // s
        ms.append((bi % 2 == 1) & (bj == bi - 1))
    return np.stack(ms).astype(np.float32)


def _np_gla_sel():
    i = np.arange(CHUNK)[:, None]
    m = np.arange(CHUNK)[None, :]
    blocks = [(m <= i), (m > i)]
    for s in _level_sizes():
        p = i // s
        odd = (p % 2 == 1)
        b_odd = p * s - 1
        b_even = (p + 1) * s - 1
        blocks.append(np.where(odd, (m > b_odd) & (m <= i), (m > i) & (m <= b_even)))
    return np.concatenate(blocks, axis=0).astype(np.float32)


def _inproj_kernel(x_ref, g_ref, wm_ref, ws_ref, om_ref, os_ref, h_ref):
    @pl.when(pl.program_id(1) == 0)
    def _():
        x = x_ref[...]
        h = x * lax.rsqrt(jnp.mean(x * x, axis=-1, keepdims=True) + RMS_EPS) * g_ref[...]
        hb = h.astype(BF16)
        h_ref[...] = hb
        os_ref[...] = jnp.dot(hb, ws_ref[...], preferred_element_type=F32)

    om_ref[...] = jnp.dot(h_ref[...], wm_ref[...], preferred_element_type=F32)


def _inproj(x, g, w_main, w_small, tm=1024, tn=HEAD_W):
    t, d = x.shape
    grid = (t // tm, MAIN_W // tn)
    return pl.pallas_call(
        _inproj_kernel,
        grid=grid,
        in_specs=[
            pl.BlockSpec((tm, d), lambda i, j: (i, 0)),
            pl.BlockSpec((1, d), lambda i, j: (0, 0)),
            pl.BlockSpec((d, tn), lambda i, j: (0, j)),
            pl.BlockSpec((d, LANES), lambda i, j: (0, 0)),
        ],
        out_specs=[
            pl.BlockSpec((tm, tn), lambda i, j: (i, j)),
            pl.BlockSpec((tm, LANES), lambda i, j: (i, 0)),
        ],
        out_shape=[jax.ShapeDtypeStruct((t, MAIN_W), F32),
                   jax.ShapeDtypeStruct((t, LANES), F32)],
        scratch_shapes=[pltpu.VMEM((tm, d), BF16)],
        compiler_params=_cparams(("parallel", "arbitrary")),
        name="inproj",
    )(x, g, w_main, w_small)


def _gla_kernel(q_ref, k_ref, v_ref, r_ref, sm_ref, wgu_ref, bg_ref, ng_ref, sel_ref, msk_ref,
                o_ref, st_ref, *, n_chunks):
    @pl.when(pl.program_id(1) == 0)
    def _():
        st_ref[...] = jnp.zeros_like(st_ref)

    levels = _level_sizes()
    sel = sel_ref[...]
    eye = msk_ref[0] - msk_ref[1]
    z = _dot3(sm_ref[...], wgu_ref[...]) + bg_ref[...]
    la_all = (jnp.minimum(z, 0.0) - jnp.log1p(jnp.exp(-jnp.abs(z)))) * (1.0 / GLA_GATE_NORM)
    ng = ng_ref[...]
    for c in range(n_chunks):
        rows = slice(c * CHUNK, (c + 1) * CHUNK)
        for h in range(GLA_HEADS):
            cols = slice(h * GLA_DK, (h + 1) * GLA_DK)
            la = la_all[rows, cols]
            e_all = jnp.exp(_dot_sel(sel, la))
            q = q_ref[rows, cols] * (GLA_DK ** -0.5)
            k = k_ref[rows, cols]
            v = v_ref[rows, cols]
            e_q = e_all[0:CHUNK]
            e_k = e_all[CHUNK:2 * CHUNK]
            scores = eye * _dot_nt(q, k)
            for li in range(len(levels)):
                e_l = e_all[(2 + li) * CHUNK:(3 + li) * CHUNK]
                scores = scores + msk_ref[2 + li] * _dot_nt(q * e_l, k * e_l)
            st = st_ref[h]
            o = _dot_nt(q * e_q, st) + _dot(scores, v)
            e_last = e_q[CHUNK - 1:CHUNK, :]
            st_ref[h] = st * e_last + _dot_tn(v, k * e_k)
            on = o * lax.rsqrt(jnp.mean(o * o, axis=-1, keepdims=True) + RMS_EPS) * ng
            o_ref[rows, cols] = on * _silu(r_ref[rows, cols])


def _gla(main, small, wgu_pad, b_gate, norm_g, b, s, ts=256):
    t = main.shape[0]
    nst = s // ts
    row = lambda bi, si: bi * nst + si
    sel = jnp.asarray(_np_gla_sel(), BF16)
    msk = jnp.asarray(_np_masks(), F32)

    def col(cb):
        return pl.BlockSpec((ts, HEAD_W), lambda bi, si: (row(bi, si), cb))

    const2 = lambda bi, si: (0, 0)
    return pl.pallas_call(
        functools.partial(_gla_kernel, n_chunks=ts // CHUNK),
        grid=(b, nst),
        in_specs=[
            col(CB_GLA_Q), col(CB_GLA_K), col(CB_GLA_V), col(CB_GLA_R),
            pl.BlockSpec((ts, LANES), lambda bi, si: (row(bi, si), 0)),
            pl.BlockSpec(wgu_pad.shape, const2),
            pl.BlockSpec(b_gate.shape, const2),
            pl.BlockSpec(norm_g.shape, const2),
            pl.BlockSpec(sel.shape, const2),
            pl.BlockSpec(msk.shape, lambda bi, si: (0, 0, 0)),
        ],
        out_specs=pl.BlockSpec((ts, HEAD_W), lambda bi, si: (row(bi, si), 0)),
        out_shape=jax.ShapeDtypeStruct((t, HEAD_W), F32),
        scratch_shapes=[pltpu.VMEM((GLA_HEADS, GLA_DK, GLA_DK), F32)],
        compiler_params=_cparams(("parallel", "arbitrary")),
        name="gla",
    )(main, main, main, main, small, wgu_pad, b_gate, norm_g, sel, msk)


def _att_kernel(q_ref, kp_ref, kc_ref, vp_ref, vc_ref, bias_ref, o_ref, kk_ref, vv_ref,
                *, n_chunks, left):
    band = left + CHUNK
    kk_ref[0:left, :] = kp_ref[...].astype(BF16)
    kk_ref[left:2 * left, :] = kc_ref[...].astype(BF16)
    vv_ref[0:left, :] = vp_ref[...].astype(BF16)
    vv_ref[left:2 * left, :] = vc_ref[...].astype(BF16)
    first = pl.program_id(1) == 0
    lane = lax.broadcasted_iota(jnp.int32, (CHUNK, LANES), 1)
    low_half = lane < ATT_HD
    jband = lax.broadcasted_iota(jnp.int32, (CHUNK, band), 1)

    def chunk_body(c, carry):
        r0 = pl.multiple_of(c * CHUNK, CHUNK)
        thr = jnp.where(first, left - r0, 0)
        valid = jband >= thr
        for hp in range(ATT_HEADS // 2):
            cols = slice(hp * LANES, (hp + 1) * LANES)
            q2 = q_ref[pl.ds(r0, CHUNK), cols] * (ATT_HD ** -0.5)
            k2 = kk_ref[pl.ds(r0, band), cols]
            v2 = vv_ref[pl.ds(r0, band), cols]
            outs = []
            for par in range(2):
                qm = jnp.where(low_half if par == 0 else ~low_half, q2, 0.0)
                sc = _dot_nt(qm, k2) + bias_ref[2 * hp + par]
                sc = jnp.where(valid, sc, -jnp.inf)
                m = jnp.max(sc, axis=-1, keepdims=True)
                p = jnp.exp(sc - m)
                l = jnp.sum(p, axis=-1, keepdims=True)
                outs.append(_dot(p, v2) / l)
            o_ref[pl.ds(r0, CHUNK), cols] = jnp.where(low_half, outs[0], outs[1])
        return carry

    lax.fori_loop(0, n_chunks, chunk_body, 0)


def _att(main, bias, b, s):
    t = main.shape[0]
    left = ATT_LEFT_CHUNKS * CHUNK
    tq = left
    nst = s // tq
    cur = lambda cb: pl.BlockSpec((tq, HEAD_W), lambda bi, si: (bi * nst + si, cb))
    prev = lambda cb: pl.BlockSpec((tq, HEAD_W),
                                   lambda bi, si: (bi * nst + jnp.maximum(si - 1, 0), cb))
    return pl.pallas_call(
        functools.partial(_att_kernel, n_chunks=tq // CHUNK, left=left),
        grid=(b, nst),
        in_specs=[
            cur(CB_ATT_Q), prev(CB_ATT_K), cur(CB_ATT_K), prev(CB_ATT_V), cur(CB_ATT_V),
            pl.BlockSpec(bias.shape, lambda bi, si: (0, 0, 0)),
        ],
        out_specs=pl.BlockSpec((tq, HEAD_W), lambda bi, si: (bi * nst + si, 0)),
        out_shape=jax.ShapeDtypeStruct((t, HEAD_W), F32),
        scratch_shapes=[pltpu.VMEM((2 * left, HEAD_W), BF16),
                        pltpu.VMEM((2 * left, HEAD_W), BF16)],
        compiler_params=_cparams(("parallel", "arbitrary")),
        name="band_att",
    )(main, main, main, main, main, bias)


def _gdn_kernel(q_ref, k_ref, v_ref, z_ref, sm_ref, cw_ref, alog_ref, dtb_ref, ng_ref, msk_ref,
                o_ref, st_ref, xb_ref, *, n_chunks):
    ts = n_chunks * CHUNK
    pad = 8

    @pl.when(pl.program_id(1) == 0)
    def _():
        st_ref[...] = jnp.zeros_like(st_ref)
        xb_ref[0:pad, :] = jnp.zeros((pad, 3 * HEAD_W), F32)

    xb_ref[pad:pad + ts, 0:HEAD_W] = q_ref[...]
    xb_ref[pad:pad + ts, HEAD_W:2 * HEAD_W] = k_ref[...]
    xb_ref[pad:pad + ts, 2 * HEAD_W:3 * HEAD_W] = v_ref[...]
    acc = None
    for j in range(CONV_W):
        off = pad - (CONV_W - 1) + j
        term = xb_ref[off:off + ts, :] * cw_ref[j:j + 1, :]
        acc = term if acc is None else acc + term
    xb_ref[0:pad, :] = xb_ref[ts:ts + pad, :]
    qkv = _silu(acc)

    sm = sm_ref[...]
    la_all = -jnp.exp(alog_ref[...]) * _softplus(sm + dtb_ref[...])
    beta_all = _sigmoid(sm)
    incl = msk_ref[0]
    strict = msk_ref[1]
    tril_b = incl.astype(BF16)
    eye = incl - strict
    ng = ng_ref[...]
    levels = _level_sizes()
    for c in range(n_chunks):
        rows = slice(c * CHUNK, (c + 1) * CHUNK)
        gc_all = _dot_sel(tril_b, la_all[rows])
        for h in range(GDN_HEADS):
            cols = slice(h * GDN_DK, (h + 1) * GDN_DK)
            la_col = la_all[rows, SMALL_A + h:SMALL_A + h + 1]
            beta = beta_all[rows, SMALL_B + h:SMALL_B + h + 1]
            gc = gc_all[:, SMALL_A + h:SMALL_A + h + 1]
            gc_last = gc[CHUNK - 1:CHUNK, :]
            dmat = _dot_sel(tril_b, la_col * strict)
            gamma = jnp.exp(dmat)
            q = qkv[rows, h * GDN_DK:(h + 1) * GDN_DK]
            k = qkv[rows, HEAD_W + h * GDN_DK:HEAD_W + (h + 1) * GDN_DK]
            v = qkv[rows, 2 * HEAD_W + h * GDN_DK:2 * HEAD_W + (h + 1) * GDN_DK]
            qn = q * lax.rsqrt(jnp.sum(q * q, axis=-1, keepdims=True) + RMS_EPS) * (GDN_DK ** -0.5)
            kn = k * lax.rsqrt(jnp.sum(k * k, axis=-1, keepdims=True) + RMS_EPS)
            kb = kn * beta
            a_mat = strict * (_dot_nt(kb, kn) * gamma)
            rhs = jnp.concatenate([v * beta, kb * jnp.exp(gc)], axis=1)
            p_inv = eye - a_mat * msk_ref[2 + len(levels) - 1]
            for li in range(len(levels) - 2, -1, -1):
                cross = a_mat * msk_ref[2 + li]
                p_inv = p_inv - _dot(_dot(p_inv, cross), p_inv)
            sol = _dot(p_inv, rhs)
            u = sol[:, :GDN_DK]
            w = sol[:, GDN_DK:]
            attn = incl * (_dot_nt(qn, kn) * gamma)
            st = st_ref[h]
            v_new = u - _dot_nt(w, st)
            o = _dot_nt(qn * jnp.exp(gc), st) + _dot(attn, v_new)
            st_ref[h] = st * jnp.exp(gc_last) + _dot_tn(v_new, kn * jnp.exp(gc_last - gc))
            on = o * lax.rsqrt(jnp.mean(o * o, axis=-1, keepdims=True) + RMS_EPS) * ng
            o_ref[rows, cols] = on * _silu(z_ref[rows, cols])


def _gdn(main, small, conv_w, alog_vec, dtb_vec, norm_g, b, s, ts=256):
    t = main.shape[0]
    nst = s // ts
    row = lambda bi, si: bi * nst + si
    msk = jnp.asarray(_np_masks(), F32)

    def col(cb):
        return pl.BlockSpec((ts, HEAD_W), lambda bi, si: (row(bi, si), cb))

    const2 = lambda bi, si: (0, 0)
    return pl.pallas_call(
        functools.partial(_gdn_kernel, n_chunks=ts // CHUNK),
        grid=(b, nst),
        in_specs=[
            col(CB_GDN_Q), col(CB_GDN_K), col(CB_GDN_V), col(CB_GDN_Z),
            pl.BlockSpec((ts, LANES), lambda bi, si: (row(bi, si), 0)),
            pl.BlockSpec(conv_w.shape, const2),
            pl.BlockSpec(alog_vec.shape, const2),
            pl.BlockSpec(dtb_vec.shape, const2),
            pl.BlockSpec(norm_g.shape, const2),
            pl.BlockSpec(msk.shape, lambda bi, si: (0, 0, 0)),
        ],
        out_specs=pl.BlockSpec((ts, HEAD_W), lambda bi, si: (row(bi, si), 0)),
        out_shape=jax.ShapeDtypeStruct((t, HEAD_W), F32),
        scratch_shapes=[pltpu.VMEM((GDN_HEADS, GDN_DK, GDN_DK), F32),
                        pltpu.VMEM((ts + 8, 3 * HEAD_W), F32)],
        compiler_params=_cparams(("parallel", "arbitrary")),
        name="gdn",
    )(main, main, main, main, small, conv_w, alog_vec, dtb_vec, norm_g, msk)


def _merge_kernel(x_ref, m0_ref, m1_ref, m2_ref, og_ref, oa_ref, od_ref, wg_ref, wa_ref, wd_ref,
                  wo_ref, g2_ref, x1_ref, h2_ref):
    y = (_sigmoid(m0_ref[...]) * jnp.dot(og_ref[...].astype(BF16), wg_ref[...], preferred_element_type=F32)
         + _sigmoid(m1_ref[...]) * jnp.dot(oa_ref[...].astype(BF16), wa_ref[...], preferred_element_type=F32)
         + _sigmoid(m2_ref[...]) * jnp.dot(od_ref[...].astype(BF16), wd_ref[...], preferred_element_type=F32))
    x1 = x_ref[...] + jnp.dot(y.astype(BF16), wo_ref[...], preferred_element_type=F32)
    x1_ref[...] = x1
    h2 = x1 * lax.rsqrt(jnp.mean(x1 * x1, axis=-1, keepdims=True) + RMS_EPS) * g2_ref[...]
    h2_ref[...] = h2.astype(BF16)


def _merge(x, main, o_gla, o_att, o_gdn, wg, wa, wd, wo, g2, tm=256):
    t, d = x.shape
    rowb = lambda i: (i, 0)
    const = lambda i: (0, 0)
    return pl.pallas_call(
        _merge_kernel,
        grid=(t // tm,),
        in_specs=[
            pl.BlockSpec((tm, d), rowb),
            pl.BlockSpec((tm, d), lambda i: (i, 0)),
            pl.BlockSpec((tm, d), lambda i: (i, 1)),
            pl.BlockSpec((tm, d), lambda i: (i, 2)),
            pl.BlockSpec((tm, HEAD_W), rowb),
            pl.BlockSpec((tm, HEAD_W), rowb),
            pl.BlockSpec((tm, HEAD_W), rowb),
            pl.BlockSpec(wg.shape, const),
            pl.BlockSpec(wa.shape, const),
            pl.BlockSpec(wd.shape, const),
            pl.BlockSpec(wo.shape, const),
            pl.BlockSpec(g2.shape, const),
        ],
        out_specs=[pl.BlockSpec((tm, d), rowb), pl.BlockSpec((tm, d), rowb)],
        out_shape=[jax.ShapeDtypeStruct((t, d), F32), jax.ShapeDtypeStruct((t, d), BF16)],
        compiler_params=_cparams(("parallel",)),
        name="merge_out",
    )(x, main, main, main, o_gla, o_att, o_gdn, wg, wa, wd, wo, g2)


def _ffn_kernel(x1_ref, h2_ref, wgate_ref, wup_ref, wdown_ref, gf_ref, o_ref, acc_ref, *, final_norm):
    j = pl.program_id(1)

    @pl.when(j == 0)
    def _():
        acc_ref[...] = x1_ref[...]

    h2 = h2_ref[...]
    gate = jnp.dot(h2, wgate_ref[...], preferred_element_type=F32)
    up = jnp.dot(h2, wup_ref[...], preferred_element_type=F32)
    acc_ref[...] += jnp.dot((_silu(gate) * up).astype(BF16), wdown_ref[...],
                            preferred_element_type=F32)

    @pl.when(j == pl.num_programs(1) - 1)
    def _():
        x2 = acc_ref[...]
        if final_norm:
            x2 = x2 * lax.rsqrt(jnp.mean(x2 * x2, axis=-1, keepdims=True) + RMS_EPS) * gf_ref[...]
        o_ref[...] = x2


def _ffn(x1, h2, w_in, w_out, gf, final_norm, tm=512, n_h=2):
    t, d = x1.shape
    hidden = w_out.shape[0]
    th = hidden // n_h
    return pl.pallas_call(
        functools.partial(_ffn_kernel, final_norm=final_norm),
        grid=(t // tm, n_h),
        in_specs=[
            pl.BlockSpec((tm, d), lambda i, j: (i, 0)),
            pl.BlockSpec((tm, d), lambda i, j: (i, 0)),
            pl.BlockSpec((d, th), lambda i, j: (0, j)),
            pl.BlockSpec((d, th), lambda i, j: (0, j + n_h)),
            pl.BlockSpec((th, d), lambda i, j: (j, 0)),
            pl.BlockSpec(gf.shape, lambda i, j: (0, 0)),
        ],
        out_specs=pl.BlockSpec((tm, d), lambda i, j: (i, 0)),
        out_shape=jax.ShapeDtypeStruct((t, d), F32),
        scratch_shapes=[pltpu.VMEM((tm, d), F32)],
        compiler_params=_cparams(("parallel", "arbitrary")),
        name="ffn",
    )(x1, h2, w_in, w_in, w_out, gf)


def _split_w_in(w_in):
    widths = (512, 512, 512, 512, GLA_GATE_RANK, 512, 512, 512, 512, 512, 512, 512,
              GDN_HEADS, GDN_HEADS, w_in.shape[1] - (11 * 512 + GLA_GATE_RANK + 2 * GDN_HEADS))
    bounds = np.cumsum(widths)[:-1]
    (gq, gk, gv, gr, glr, aq, ak, av, dq, dk, dv, dz, da, db, mg) = jnp.split(w_in, bounds, axis=1)
    main = jnp.concatenate([mg, gq, gk, gv, gr, aq, ak, av, dq, dk, dv, dz], axis=1)
    small = jnp.concatenate([glr, da, db], axis=1)
    small = jnp.pad(small, ((0, 0), (0, LANES - small.shape[1])))
    return main.astype(BF16), small.astype(BF16)


def _band_bias(rel_bias):
    left = ATT_LEFT_CHUNKS * CHUNK
    band = left + CHUNK
    rel = np.arange(CHUNK)[:, None] + left - np.arange(band)[None, :]
    idx = np.clip(rel, -(CHUNK - 1), REL_MAX) + (CHUNK - 1)
    return rel_bias[:, idx].astype(F32)


def _lane_vec(vals, offset):
    out = jnp.zeros((1, LANES), F32)
    return lax.dynamic_update_slice(out, vals.reshape(1, -1).astype(F32), (0, offset))


def kernel(x, mix_norm_g, w_in, gla_w_gate_up, gla_b_gate, gla_norm_g, att_rel_bias, gdn_conv_w,
           gdn_a_log, gdn_dt_bias, gdn_norm_g, w_branch_gla, w_branch_att, w_branch_gdn, w_out,
           ffn_norm_g, w_ffn_in, w_ffn_out, final_norm_g):
    b, s, d = x.shape
    depth = w_in.shape[0]
    xt = x.reshape(b * s, d)
    for l in range(depth):
        w_main, w_small = _split_w_in(w_in[l])
        main, small = _inproj(xt, mix_norm_g[l].reshape(1, d), w_main, w_small)
        wgu_pad = jnp.pad(gla_w_gate_up[l], ((0, LANES - GLA_GATE_RANK), (0, 0)))
        o_gla = _gla(main, small, wgu_pad, gla_b_gate[l].reshape(1, -1),
                     gla_norm_g[l].reshape(1, -1), b, s)
        o_att = _att(main, _band_bias(att_rel_bias[l]), b, s)
        o_gdn = _gdn(main, small, gdn_conv_w[l], _lane_vec(gdn_a_log[l], SMALL_A),
                     _lane_vec(gdn_dt_bias[l], SMALL_A), gdn_norm_g[l].reshape(1, -1), b, s)
        x1, h2 = _merge(xt, main, o_gla, o_att, o_gdn, w_branch_gla[l].astype(BF16),
                        w_branch_att[l].astype(BF16), w_branch_gdn[l].astype(BF16),
                        w_out[l].astype(BF16), ffn_norm_g[l].reshape(1, d))
        xt = _ffn(x1, h2, w_ffn_in[l].astype(BF16), w_ffn_out[l].astype(BF16),
                  final_norm_g.reshape(1, d), final_norm=(l == depth - 1))
    return xt.reshape(b, s, d)
```

```python
import functools

import numpy as np
import jax
import jax.numpy as jnp
from jax import lax
from jax.experimental import pallas as pl
from jax.experimental.pallas import tpu as pltpu

F32 = jnp.float32
BF16 = jnp.bfloat16

CHUNK = 64
RMS_EPS = 1e-6
GLA_HEADS = 4
GLA_DK = 128
GLA_GATE_RANK = 16
GLA_GATE_NORM = 16.0
ATT_HEADS = 8
ATT_HD = 64
ATT_LEFT_CHUNKS = 8
REL_MAX = 256
GDN_HEADS = 4
GDN_DK = 128
CONV_W = 4
HEAD_W = 512
LANES = 128
SMALL_A = GLA_GATE_RANK
SMALL_B = GLA_GATE_RANK + GDN_HEADS

CB_GLA_Q, CB_GLA_K, CB_GLA_V, CB_GLA_R = 6, 7, 8, 9
CB_ATT_Q, CB_ATT_K, CB_ATT_V = 10, 11, 12
CB_GDN_Q, CB_GDN_K, CB_GDN_V, CB_GDN_Z = 13, 14, 15, 16
MAIN_W = 17 * HEAD_W

VMEM_LIMIT = 56 * 1024 * 1024


def _cparams(sem):
    return pltpu.CompilerParams(dimension_semantics=sem, vmem_limit_bytes=VMEM_LIMIT)


def _dot(a, b):
    return jnp.dot(a.astype(BF16), b.astype(BF16), preferred_element_type=F32)


def _dot_nt(a, b):
    return lax.dot_general(a.astype(BF16), b.astype(BF16), (((1,), (1,)), ((), ())),
                           preferred_element_type=F32)


def _dot_tn(a, b):
    return lax.dot_general(a.astype(BF16), b.astype(BF16), (((0,), (0,)), ((), ())),
                           preferred_element_type=F32)


def _split(x):
    hi = x.astype(BF16)
    lo = (x - hi.astype(F32)).astype(BF16)
    return hi, lo


def _dot_sel(m_bf16, x):
    hi, lo = _split(x)
    return (jnp.dot(m_bf16, hi, preferred_element_type=F32)
            + jnp.dot(m_bf16, lo, preferred_element_type=F32))


def _dot3(a, b):
    ah, al = _split(a)
    bh, bl = _split(b)
    return (jnp.dot(ah, bh, preferred_element_type=F32)
            + jnp.dot(al, bh, preferred_element_type=F32)
            + jnp.dot(ah, bl, preferred_element_type=F32))


def _sigmoid(x):
    return 1.0 / (1.0 + jnp.exp(-x))


def _silu(x):
    return x * _sigmoid(x)


def _softplus(x):
    return jnp.maximum(x, 0.0) + jnp.log1p(jnp.exp(-jnp.abs(x)))


def _level_sizes():
    s, out = CHUNK // 2, []
    while s >= 1:
        out.append(s)
        s //= 2
    return out


def _np_masks():
    i = np.arange(CHUNK)[:, None]
    j = np.arange(CHUNK)[None, :]
    ms = [(j <= i), (j < i)]
    for s in _level_sizes():
        bi, bj = i // s, j // s
        ms.append((bi % 2 == 1) & (bj == bi - 1))
    return np.stack(ms).astype(np.float32)


def _np_gla_sel():
    i = np.arange(CHUNK)[:, None]
    m = np.arange(CHUNK)[None, :]
    blocks = [(m <= i), (m > i)]
    for s in _level_sizes():
        p = i // s
        odd = (p % 2 == 1)
        b_odd = p * s - 1
        b_even = (p + 1) * s - 1
        blocks.append(np.where(odd, (m > b_odd) & (m <= i), (m > i) & (m <= b_even)))
    return np.concatenate(blocks, axis=0).astype(np.float32)


def _inproj_kernel(x_ref, g_ref, wm_ref, ws_ref, om_ref, os_ref, h_ref):
    @pl.when(pl.program_id(1) == 0)
    def _():
        x = x_ref[...]
        h = x * lax.rsqrt(jnp.mean(x * x, axis=-1, keepdims=True) + RMS_EPS) * g_ref[...]
        hb = h.astype(BF16)
        h_ref[...] = hb
        os_ref[...] = jnp.dot(hb, ws_ref[...], preferred_element_type=F32)

    om_ref[...] = jnp.dot(h_ref[...], wm_ref[...], preferred_element_type=F32)


def _inproj(x, g, w_main, w_small, tm=1024, tn=HEAD_W):
    t, d = x.shape
    grid = (t // tm, MAIN_W // tn)
    return pl.pallas_call(
        _inproj_kernel,
        grid=grid,
        in_specs=[
            pl.BlockSpec((tm, d), lambda i, j: (i, 0)),
            pl.BlockSpec((1, d), lambda i, j: (0, 0)),
            pl.BlockSpec((d, tn), lambda i, j: (0, j)),
            pl.BlockSpec((d, LANES), lambda i, j: (0, 0)),
        ],
        out_specs=[
            pl.BlockSpec((tm, tn), lambda i, j: (i, j)),
            pl.BlockSpec((tm, LANES), lambda i, j: (i, 0)),
        ],
        out_shape=[jax.ShapeDtypeStruct((t, MAIN_W), F32),
                   jax.ShapeDtypeStruct((t, LANES), F32)],
        scratch_shapes=[pltpu.VMEM((tm, d), BF16)],
        compiler_params=_cparams(("parallel", "arbitrary")),
        name="inproj",
    )(x, g, w_main, w_small)


def _gla_kernel(q_ref, k_ref, v_ref, r_ref, sm_ref, wgu_ref, bg_ref, ng_ref, sel_ref, msk_ref,
                o_ref, st_ref, e_ref, *, n_chunks, group):
    @pl.when(pl.program_id(1) == 0)
    def _():
        st_ref[...] = jnp.zeros_like(st_ref)

    n_lv = len(_level_sizes())
    sel = sel_ref[...]
    eye2 = msk_ref[0] - msk_ref[1]
    z = _dot3(sm_ref[...], wgu_ref[...]) + bg_ref[...]
    la_all = (jnp.minimum(z, 0.0) - jnp.log1p(jnp.exp(-jnp.abs(z)))) * (1.0 / GLA_GATE_NORM)
    ng = ng_ref[...]
    zero_b = jnp.zeros((CHUNK, GLA_DK), BF16)

    def head_blocks(x2):
        return jnp.concatenate([jnp.concatenate([x2[:, :GLA_DK], zero_b], axis=1),
                                jnp.concatenate([zero_b, x2[:, GLA_DK:]], axis=1)], axis=0)

    sv, upd, qe, e_last = {}, {}, {}, {}
    for c0 in range(0, n_chunks, group):
        chunks = range(c0, min(c0 + group, n_chunks))
        for c in chunks:
            e_ref[c] = jnp.exp(_dot_sel(sel, la_all[c * CHUNK:(c + 1) * CHUNK]))
        units = [(c, hp) for c in chunks for hp in range(GLA_HEADS // 2)]
        lhs, rhs, vb, kd = {}, {}, {}, {}
        for (c, hp) in units:
            rows = slice(c * CHUNK, (c + 1) * CHUNK)
            cols2 = slice(hp * 2 * GLA_DK, (hp + 1) * 2 * GLA_DK)
            q2 = q_ref[rows, cols2] * (GLA_DK ** -0.5)
            k2 = k_ref[rows, cols2]
            ls, rs = [q2.astype(BF16)], [head_blocks(k2.astype(BF16))]
            for li in range(n_lv):
                e_l = e_ref[c, (2 + li) * CHUNK:(3 + li) * CHUNK, cols2]
                ls.append((q2 * e_l).astype(BF16))
                rs.append(head_blocks((k2 * e_l).astype(BF16)))
            lhs[(c, hp)], rhs[(c, hp)] = ls, rs
            e_q = e_ref[c, 0:CHUNK, cols2]
            qe[(c, hp)] = (q2 * e_q).astype(BF16)
            e_last[(c, hp)] = e_q[CHUNK - 1:CHUNK, :]
            kd[(c, hp)] = (k2 * e_ref[c, CHUNK:2 * CHUNK, cols2]).astype(BF16)
            vb[(c, hp)] = v_ref[rows, cols2].astype(BF16)
        raw = {u: [_dot_nt(a, b_) for a, b_ in zip(lhs[u], rhs[u])] for u in units}
        for u in units:
            upd[u] = [_dot_tn(vb[u][:, h * GLA_DK:(h + 1) * GLA_DK], kd[u][:, h * GLA_DK:(h + 1) * GLA_DK])
                      for h in range(2)]
        for u in units:
            sc = eye2 * raw[u][0]
            for li in range(n_lv):
                sc = sc + msk_ref[2 + li] * raw[u][1 + li]
            sv[u] = jnp.dot(sc.astype(BF16), head_blocks(vb[u]), preferred_element_type=F32)

    for c in range(n_chunks):
        rows = slice(c * CHUNK, (c + 1) * CHUNK)
        for hp in range(GLA_HEADS // 2):
            u = (c, hp)
            for hh in range(2):
                h = 2 * hp + hh
                cols = slice(h * GLA_DK, (h + 1) * GLA_DK)
                lc = slice(hh * GLA_DK, (hh + 1) * GLA_DK)
                st = st_ref[h]
                o = _dot_nt(qe[u][:, lc], st) + sv[u][:, lc]
                st_ref[h] = st * e_last[u][:, lc] + upd[u][hh]
                on = o * lax.rsqrt(jnp.mean(o * o, axis=-1, keepdims=True) + RMS_EPS) * ng
                o_ref[rows, cols] = on * _silu(r_ref[rows, cols])


def _gla(main, small, wgu_pad, b_gate, norm_g, b, s, ts=256, group=2):
    t = main.shape[0]
    nst = s // ts
    row = lambda bi, si: bi * nst + si
    sel = jnp.asarray(_np_gla_sel(), BF16)
    msk = jnp.asarray(np.tile(_np_masks(), (1, 1, 2)), F32)

    def col(cb):
        return pl.BlockSpec((ts, HEAD_W), lambda bi, si: (row(bi, si), cb))

    const2 = lambda bi, si: (0, 0)
    return pl.pallas_call(
        functools.partial(_gla_kernel, n_chunks=ts // CHUNK, group=group),
        grid=(b, nst),
        in_specs=[
            col(CB_GLA_Q), col(CB_GLA_K), col(CB_GLA_V), col(CB_GLA_R),
            pl.BlockSpec((ts, LANES), lambda bi, si: (row(bi, si), 0)),
            pl.BlockSpec(wgu_pad.shape, const2),
            pl.BlockSpec(b_gate.shape, const2),
            pl.BlockSpec(norm_g.shape, const2),
            pl.BlockSpec(sel.shape, const2),
            pl.BlockSpec(msk.shape, lambda bi, si: (0, 0, 0)),
        ],
        out_specs=pl.BlockSpec((ts, HEAD_W), lambda bi, si: (row(bi, si), 0)),
        out_shape=jax.ShapeDtypeStruct((t, HEAD_W), F32),
        scratch_shapes=[pltpu.VMEM((GLA_HEADS, GLA_DK, GLA_DK), F32),
                        pltpu.VMEM((ts // CHUNK, sel.shape[0], HEAD_W), F32)],
        compiler_params=_cparams(("parallel", "arbitrary")),
        name="gla",
    )(main, main, main, main, small, wgu_pad, b_gate, norm_g, sel, msk)


def _att_kernel(q_ref, kp_ref, kc_ref, vp_ref, vc_ref, bias_ref, o_ref, kk_ref, vv_ref,
                *, n_chunks, left):
    band = left + CHUNK
    kk_ref[0:left, :] = kp_ref[...].astype(BF16)
    kk_ref[left:2 * left, :] = kc_ref[...].astype(BF16)
    vv_ref[0:left, :] = vp_ref[...].astype(BF16)
    vv_ref[left:2 * left, :] = vc_ref[...].astype(BF16)
    first = pl.program_id(1) == 0
    lane = lax.broadcasted_iota(jnp.int32, (CHUNK, LANES), 1)
    low_half = lane < ATT_HD

    def chunk_body(c, carry, *, masked):
        r0 = pl.multiple_of(c * CHUNK, CHUNK)
        scores = []
        for hp in range(ATT_HEADS // 2):
            cols = slice(hp * LANES, (hp + 1) * LANES)
            q2 = q_ref[pl.ds(r0, CHUNK), cols] * (ATT_HD ** -0.5)
            k2 = kk_ref[pl.ds(r0, band), cols]
            for par in range(2):
                qm = jnp.where(low_half if par == 0 else ~low_half, q2, 0.0).astype(BF16)
                scores.append(_dot_nt(qm, k2))
        if masked:
            jband = lax.broadcasted_iota(jnp.int32, (CHUNK, band), 1)
            valid = jband >= left - r0
        probs = []
        for h in range(ATT_HEADS):
            sc = scores[h] + bias_ref[h]
            if masked:
                sc = jnp.where(valid, sc, -jnp.inf)
            m = jnp.max(sc, axis=-1, keepdims=True)
            p = jnp.exp(sc - m)
            probs.append((p.astype(BF16), 1.0 / jnp.sum(p, axis=-1, keepdims=True)))
        for hp in range(ATT_HEADS // 2):
            cols = slice(hp * LANES, (hp + 1) * LANES)
            v2 = vv_ref[pl.ds(r0, band), cols]
            outs = [jnp.dot(probs[2 * hp + par][0], v2, preferred_element_type=F32) * probs[2 * hp + par][1]
                    for par in range(2)]
            o_ref[pl.ds(r0, CHUNK), cols] = jnp.where(low_half, outs[0], outs[1])
        return carry

    @pl.when(first)
    def _():
        lax.fori_loop(0, n_chunks, functools.partial(chunk_body, masked=True), 0)

    @pl.when(jnp.logical_not(first))
    def _():
        lax.fori_loop(0, n_chunks, functools.partial(chunk_body, masked=False), 0)


def _att(main, bias, b, s):
    t = main.shape[0]
    left = ATT_LEFT_CHUNKS * CHUNK
    tq = left
    nst = s // tq
    cur = lambda cb: pl.BlockSpec((tq, HEAD_W), lambda bi, si: (bi * nst + si, cb))
    prev = lambda cb: pl.BlockSpec((tq, HEAD_W),
                                   lambda bi, si: (bi * nst + jnp.maximum(si - 1, 0), cb))
    return pl.pallas_call(
        functools.partial(_att_kernel, n_chunks=tq // CHUNK, left=left),
        grid=(b, nst),
        in_specs=[
            cur(CB_ATT_Q), prev(CB_ATT_K), cur(CB_ATT_K), prev(CB_ATT_V), cur(CB_ATT_V),
            pl.BlockSpec(bias.shape, lambda bi, si: (0, 0, 0)),
        ],
        out_specs=pl.BlockSpec((tq, HEAD_W), lambda bi, si: (bi * nst + si, 0)),
        out_shape=jax.ShapeDtypeStruct((t, HEAD_W), F32),
        scratch_shapes=[pltpu.VMEM((2 * left, HEAD_W), BF16),
                        pltpu.VMEM((2 * left, HEAD_W), BF16)],
        compiler_params=_cparams(("parallel", "arbitrary")),
        name="band_att",
    )(main, main, main, main, main, bias)


def _gdn_kernel(q_ref, k_ref, v_ref, z_ref, sm_ref, cw_ref, alog_ref, dtb_ref, ng_ref, msk_ref,
                o_ref, st_ref, xb_ref, qkv_ref, *, n_chunks, group):
    ts = n_chunks * CHUNK
    pad = 8

    @pl.when(pl.program_id(1) == 0)
    def _():
        st_ref[...] = jnp.zeros_like(st_ref)
        xb_ref[0:pad, :] = jnp.zeros((pad, 3 * HEAD_W), F32)

    xb_ref[pad:pad + ts, 0:HEAD_W] = q_ref[...]
    xb_ref[pad:pad + ts, HEAD_W:2 * HEAD_W] = k_ref[...]
    xb_ref[pad:pad + ts, 2 * HEAD_W:3 * HEAD_W] = v_ref[...]
    acc = None
    for j in range(CONV_W):
        off = pad - (CONV_W - 1) + j
        term = xb_ref[off:off + ts, :] * cw_ref[j:j + 1, :]
        acc = term if acc is None else acc + term
    xb_ref[0:pad, :] = xb_ref[ts:ts + pad, :]
    qkv_ref[...] = _silu(acc)

    sm = sm_ref[...]
    la_all = -jnp.exp(alog_ref[...]) * _softplus(sm + dtb_ref[...])
    beta_all = _sigmoid(sm)
    incl = msk_ref[0]
    strict = msk_ref[1]
    tril_b = incl.astype(BF16)
    eye = incl - strict
    ng = ng_ref[...]
    n_lv = len(_level_sizes())

    ops = {}
    for c0 in range(0, n_chunks, group):
        chunks = range(c0, min(c0 + group, n_chunks))
        pairs = [(c, h) for c in chunks for h in range(GDN_HEADS)]
        gc_alls = {c: _dot_sel(tril_b, la_all[c * CHUNK:(c + 1) * CHUNK]) for c in chunks}
        pre = {}
        for (c, h) in pairs:
            rows = slice(c * CHUNK, (c + 1) * CHUNK)
            la_col = la_all[rows, SMALL_A + h:SMALL_A + h + 1]
            beta = beta_all[rows, SMALL_B + h:SMALL_B + h + 1]
            gc = gc_alls[c][:, SMALL_A + h:SMALL_A + h + 1]
            gc_last = gc[CHUNK - 1:CHUNK, :]
            q = qkv_ref[rows, h * GDN_DK:(h + 1) * GDN_DK]
            k = qkv_ref[rows, HEAD_W + h * GDN_DK:HEAD_W + (h + 1) * GDN_DK]
            v = qkv_ref[rows, 2 * HEAD_W + h * GDN_DK:2 * HEAD_W + (h + 1) * GDN_DK]
            qn = q * lax.rsqrt(jnp.sum(q * q, axis=-1, keepdims=True) + RMS_EPS) * (GDN_DK ** -0.5)
            kn = k * lax.rsqrt(jnp.sum(k * k, axis=-1, keepdims=True) + RMS_EPS)
            kb = kn * beta
            pre[(c, h)] = dict(
                lcs=la_col * strict, qn=qn.astype(BF16), kn=kn.astype(BF16), kb=kb.astype(BF16),
                rhs=jnp.concatenate([v * beta, kb * jnp.exp(gc)], axis=1).astype(BF16),
                qg=qn * jnp.exp(gc), kd=(kn * jnp.exp(gc_last - gc)).astype(BF16),
                g_last=jnp.exp(gc_last))
        mm = {p: (_dot_sel(tril_b, pre[p]["lcs"]), _dot_nt(pre[p]["kb"], pre[p]["kn"]),
                  _dot_nt(pre[p]["qn"], pre[p]["kn"])) for p in pairs}
        a_mat, attn, p_inv = {}, {}, {}
        for p in pairs:
            dmat, kk, qk = mm[p]
            gamma = jnp.exp(dmat)
            a_mat[p] = strict * (kk * gamma)
            attn[p] = (incl * (qk * gamma)).astype(BF16)
            p_inv[p] = eye - a_mat[p] * msk_ref[2 + n_lv - 1]
        for li in range(n_lv - 2, -1, -1):
            half = {p: _dot(p_inv[p], a_mat[p] * msk_ref[2 + li]) for p in pairs}
            p_inv = {p: p_inv[p] - _dot(half[p], p_inv[p]) for p in pairs}
        sol = {p: _dot(p_inv[p], pre[p]["rhs"]).astype(BF16) for p in pairs}
        for p in pairs:
            nw = _dot_tn(pre[p]["kd"], sol[p])
            ao = _dot(attn[p], sol[p])
            ops[p] = (nw[:, :GDN_DK], nw[:, GDN_DK:].astype(BF16), ao[:, :GDN_DK],
                      (pre[p]["qg"] - ao[:, GDN_DK:]).astype(BF16), pre[p]["g_last"])

    for c in range(n_chunks):
        rows = slice(c * CHUNK, (c + 1) * CHUNK)
        res = {}
        for h in range(GDN_HEADS):
            n_mat, w2, o2, q2, g_last = ops[(c, h)]
            res[h] = jnp.dot(jnp.concatenate([w2, q2], axis=0), st_ref[h].astype(BF16),
                             preferred_element_type=F32)
        for h in range(GDN_HEADS):
            cols = slice(h * GDN_DK, (h + 1) * GDN_DK)
            n_mat, w2, o2, q2, g_last = ops[(c, h)]
            st_ref[h] = st_ref[h] * g_last - res[h][:GDN_DK] + n_mat
            o = res[h][GDN_DK:] + o2
            on = o * lax.rsqrt(jnp.mean(o * o, axis=-1, keepdims=True) + RMS_EPS) * ng
            o_ref[rows, cols] = on * _silu(z_ref[rows, cols])


def _gdn(main, small, conv_w, alog_vec, dtb_vec, norm_g, b, s, ts=256, group=2):
    t = main.shape[0]
    nst = s // ts
    row = lambda bi, si: bi * nst + si
    msk = jnp.asarray(_np_masks(), F32)

    def col(cb):
        return pl.BlockSpec((ts, HEAD_W), lambda bi, si: (row(bi, si), cb))

    const2 = lambda bi, si: (0, 0)
    return pl.pallas_call(
        functools.partial(_gdn_kernel, n_chunks=ts // CHUNK, group=group),
        grid=(b, nst),
        in_specs=[
            col(CB_GDN_Q), col(CB_GDN_K), col(CB_GDN_V), col(CB_GDN_Z),
            pl.BlockSpec((ts, LANES), lambda bi, si: (row(bi, si), 0)),
            pl.BlockSpec(conv_w.shape, const2),
            pl.BlockSpec(alog_vec.shape, const2),
            pl.BlockSpec(dtb_vec.shape, const2),
            pl.BlockSpec(norm_g.shape, const2),
            pl.BlockSpec(msk.shape, lambda bi, si: (0, 0, 0)),
        ],
        out_specs=pl.BlockSpec((ts, HEAD_W), lambda bi, si: (row(bi, si), 0)),
        out_shape=jax.ShapeDtypeStruct((t, HEAD_W), F32),
        scratch_shapes=[pltpu.VMEM((GDN_HEADS, GDN_DK, GDN_DK), F32),
                        pltpu.VMEM((ts + 8, 3 * HEAD_W), F32),
                        pltpu.VMEM((ts, 3 * HEAD_W), F32)],
        compiler_params=_cparams(("parallel", "arbitrary")),
        name="gdn",
    )(main, main, main, main, small, conv_w, alog_vec, dtb_vec, norm_g, msk)


def _merge_kernel(x_ref, m0_ref, m1_ref, m2_ref, og_ref, oa_ref, od_ref, wg_ref, wa_ref, wd_ref,
                  wo_ref, g2_ref, x1_ref, h2_ref):
    y = (_sigmoid(m0_ref[...]) * jnp.dot(og_ref[...].astype(BF16), wg_ref[...], preferred_element_type=F32)
         + _sigmoid(m1_ref[...]) * jnp.dot(oa_ref[...].astype(BF16), wa_ref[...], preferred_element_type=F32)
         + _sigmoid(m2_ref[...]) * jnp.dot(od_ref[...].astype(BF16), wd_ref[...], preferred_element_type=F32))
    x1 = x_ref[...] + jnp.dot(y.astype(BF16), wo_ref[...], preferred_element_type=F32)
    x1_ref[...] = x1
    h2 = x1 * lax.rsqrt(jnp.mean(x1 * x1, axis=-1, keepdims=True) + RMS_EPS) * g2_ref[...]
    h2_ref[...] = h2.astype(BF16)


def _merge(x, main, o_gla, o_att, o_gdn, wg, wa, wd, wo, g2, tm=256):
    t, d = x.shape
    rowb = lambda i: (i, 0)
    const = lambda i: (0, 0)
    return pl.pallas_call(
        _merge_kernel,
        grid=(t // tm,),
        in_specs=[
            pl.BlockSpec((tm, d), rowb),
            pl.BlockSpec((tm, d), lambda i: (i, 0)),
            pl.BlockSpec((tm, d), lambda i: (i, 1)),
            pl.BlockSpec((tm, d), lambda i: (i, 2)),
            pl.BlockSpec((tm, HEAD_W), rowb),
            pl.BlockSpec((tm, HEAD_W), rowb),
            pl.BlockSpec((tm, HEAD_W), rowb),
            pl.BlockSpec(wg.shape, const),
            pl.BlockSpec(wa.shape, const),
            pl.BlockSpec(wd.shape, const),
            pl.BlockSpec(wo.shape, const),
            pl.BlockSpec(g2.shape, const),
        ],
        out_specs=[pl.BlockSpec((tm, d), rowb), pl.BlockSpec((tm, d), rowb)],
        out_shape=[jax.ShapeDtypeStruct((t, d), F32), jax.ShapeDtypeStruct((t, d), BF16)],
        compiler_params=_cparams(("parallel",)),
        name="merge_out",
    )(x, main, main, main, o_gla, o_att, o_gdn, wg, wa, wd, wo, g2)


def _ffn_kernel(x1_ref, h2_ref, wgate_ref, wup_ref, wdown_ref, gf_ref, o_ref, acc_ref, *, final_norm):
    j = pl.program_id(1)

    @pl.when(j == 0)
    def _():
        acc_ref[...] = x1_ref[...]

    h2 = h2_ref[...]
    gate = jnp.dot(h2, wgate_ref[...], preferred_element_type=F32)
    up = jnp.dot(h2, wup_ref[...], preferred_element_type=F32)
    acc_ref[...] += jnp.dot((_silu(gate) * up).astype(BF16), wdown_ref[...],
                            preferred_element_type=F32)

    @pl.when(j == pl.num_programs(1) - 1)
    def _():
        x2 = acc_ref[...]
        if final_norm:
            x2 = x2 * lax.rsqrt(jnp.mean(x2 * x2, axis=-1, keepdims=True) + RMS_EPS) * gf_ref[...]
        o_ref[...] = x2


def _ffn(x1, h2, w_in, w_out, gf, final_norm, tm=512, n_h=2):
    t, d = x1.shape
    hidden = w_out.shape[0]
    th = hidden // n_h
    return pl.pallas_call(
        functools.partial(_ffn_kernel, final_norm=final_norm),
        grid=(t // tm, n_h),
        in_specs=[
            pl.BlockSpec((tm, d), lambda i, j: (i, 0)),
            pl.BlockSpec((tm, d), lambda i, j: (i, 0)),
            pl.BlockSpec((d, th), lambda i, j: (0, j)),
            pl.BlockSpec((d, th), lambda i, j: (0, j + n_h)),
            pl.BlockSpec((th, d), lambda i, j: (j, 0)),
            pl.BlockSpec(gf.shape, lambda i, j: (0, 0)),
        ],
        out_specs=pl.BlockSpec((tm, d), lambda i, j: (i, 0)),
        out_shape=jax.ShapeDtypeStruct((t, d), F32),
        scratch_shapes=[pltpu.VMEM((tm, d), F32)],
        compiler_params=_cparams(("parallel", "arbitrary")),
        name="ffn",
    )(x1, h2, w_in, w_in, w_out, gf)


def _split_w_in(w_in):
    widths = (512, 512, 512, 512, GLA_GATE_RANK, 512, 512, 512, 512, 512, 512, 512,
              GDN_HEADS, GDN_HEADS, w_in.shape[1] - (11 * 512 + GLA_GATE_RANK + 2 * GDN_HEADS))
    bounds = np.cumsum(widths)[:-1]
    (gq, gk, gv, gr, glr, aq, ak, av, dq, dk, dv, dz, da, db, mg) = jnp.split(w_in, bounds, axis=1)
    main = jnp.concatenate([mg, gq, gk, gv, gr, aq, ak, av, dq, dk, dv, dz], axis=1)
    small = jnp.concatenate([glr, da, db], axis=1)
    small = jnp.pad(small, ((0, 0), (0, LANES - small.shape[1])))
    return main.astype(BF16), small.astype(BF16)


def _band_bias(rel_bias):
    left = ATT_LEFT_CHUNKS * CHUNK
    band = left + CHUNK
    rel = np.arange(CHUNK)[:, None] + left - np.arange(band)[None, :]
    idx = np.clip(rel, -(CHUNK - 1), REL_MAX) + (CHUNK - 1)
    return rel_bias[:, idx].astype(F32)


def _lane_vec(vals, offset):
    out = jnp.zeros((1, LANES), F32)
    return lax.dynamic_update_slice(out, vals.reshape(1, -1).astype(F32), (0, offset))


def kernel(x, mix_norm_g, w_in, gla_w_gate_up, gla_b_gate, gla_norm_g, att_rel_bias, gdn_conv_w,
           gdn_a_log, gdn_dt_bias, gdn_norm_g, w_branch_gla, w_branch_att, w_branch_gdn, w_out,
           ffn_norm_g, w_ffn_in, w_ffn_out, final_norm_g):
    b, s, d = x.shape
    depth = w_in.shape[0]
    xt = x.reshape(b * s, d)
    for l in range(depth):
        w_main, w_small = _split_w_in(w_in[l])
        main, small = _inproj(xt, mix_norm_g[l].reshape(1, d), w_main, w_small)
        wgu_pad = jnp.pad(gla_w_gate_up[l], ((0, LANES - GLA_GATE_RANK), (0, 0)))
        o_gla = _gla(main, small, wgu_pad, gla_b_gate[l].reshape(1, -1),
                     gla_norm_g[l].reshape(1, -1), b, s)
        o_att = _att(main, _band_bias(att_rel_bias[l]), b, s)
        o_gdn = _gdn(main, small, gdn_conv_w[l], _lane_vec(gdn_a_log[l], SMALL_A),
                     _lane_vec(gdn_dt_bias[l], SMALL_A), gdn_norm_g[l].reshape(1, -1), b, s)
        x1, h2 = _merge(xt, main, o_gla, o_att, o_gdn, w_branch_gla[l].astype(BF16),
                        w_branch_att[l].astype(BF16), w_branch_gdn[l].astype(BF16),
                        w_out[l].astype(BF16), ffn_norm_g[l].reshape(1, d))
        xt = _ffn(x1, h2, w_ffn_in[l].astype(BF16), w_ffn_out[l].astype(BF16),
                  final_norm_g.reshape(1, d), final_norm=(l == depth - 1))
    return xt.reshape(b, s, d)
```

```python
import functools

import numpy as np
import jax
import jax.numpy as jnp
from jax import lax
from jax.experimental import pallas as pl
from jax.experimental.pallas import tpu as pltpu

F32 = jnp.float32
BF16 = jnp.bfloat16

CHUNK = 64
RMS_EPS = 1e-6
GLA_HEADS = 4
GLA_DK = 128
GLA_GATE_RANK = 16
GLA_GATE_NORM = 16.0
ATT_HEADS = 8
ATT_HD = 64
ATT_LEFT_CHUNKS = 8
REL_MAX = 256
GDN_HEADS = 4
GDN_DK = 128
CONV_W = 4
HEAD_W = 512
LANES = 128
SMALL_A = GLA_GATE_RANK
SMALL_B = GLA_GATE_RANK + GDN_HEADS

CB_GLA_Q, CB_GLA_K, CB_GLA_V, CB_GLA_R = 6, 7, 8, 9
CB_ATT_Q, CB_ATT_K, CB_ATT_V = 10, 11, 12
CB_GDN_Q, CB_GDN_K, CB_GDN_V, CB_GDN_Z = 13, 14, 15, 16
MAIN_W = 17 * HEAD_W

VMEM_LIMIT = 56 * 1024 * 1024


def _cparams(sem):
    return pltpu.CompilerParams(dimension_semantics=sem, vmem_limit_bytes=VMEM_LIMIT)


def _dot(a, b):
    return jnp.dot(a.astype(BF16), b.astype(BF16), preferred_element_type=F32)


def _dot_nt(a, b):
    return lax.dot_general(a.astype(BF16), b.astype(BF16), (((1,), (1,)), ((), ())),
                           preferred_element_type=F32)


def _dot_tn(a, b):
    return lax.dot_general(a.astype(BF16), b.astype(BF16), (((0,), (0,)), ((), ())),
                           preferred_element_type=F32)


def _split(x):
    hi = x.astype(BF16)
    lo = (x - hi.astype(F32)).astype(BF16)
    return hi, lo


def _dot_sel(m_bf16, x):
    hi, lo = _split(x)
    return (jnp.dot(m_bf16, hi, preferred_element_type=F32)
            + jnp.dot(m_bf16, lo, preferred_element_type=F32))


def _dot3(a, b):
    ah, al = _split(a)
    bh, bl = _split(b)
    return (jnp.dot(ah, bh, preferred_element_type=F32)
            + jnp.dot(al, bh, preferred_element_type=F32)
            + jnp.dot(ah, bl, preferred_element_type=F32))


def _sigmoid(x):
    return 0.5 * jnp.tanh(0.5 * x) + 0.5


def _silu(x):
    return x * _sigmoid(x)


def _softplus(x):
    return jnp.maximum(x, 0.0) + jnp.log1p(jnp.exp(-jnp.abs(x)))


def _level_sizes():
    s, out = CHUNK // 2, []
    while s >= 1:
        out.append(s)
        s //= 2
    return out


def _np_masks():
    i = np.arange(CHUNK)[:, None]
    j = np.arange(CHUNK)[None, :]
    ms = [(j <= i), (j < i)]
    for s in _level_sizes():
        bi, bj = i // s, j // s
        ms.append((bi % 2 == 1) & (bj == bi - 1))
    return np.stack(ms).astype(np.float32)


def _np_gla_sel():
    i = np.arange(CHUNK)[:, None]
    m = np.arange(CHUNK)[None, :]
    blocks = [(m <= i), (m > i)]
    for s in _level_sizes():
        p = i // s
        odd = (p % 2 == 1)
        b_odd = p * s - 1
        b_even = (p + 1) * s - 1
        blocks.append(np.where(odd, (m > b_odd) & (m <= i), (m > i) & (m <= b_even)))
    return np.concatenate(blocks, axis=0).astype(np.float32)


def _inproj_kernel(x_ref, g_ref, wm_ref, ws_ref, om_ref, os_ref, h_ref):
    @pl.when(pl.program_id(1) == 0)
    def _():
        x = x_ref[...]
        h = x * lax.rsqrt(jnp.mean(x * x, axis=-1, keepdims=True) + RMS_EPS) * g_ref[...]
        hb = h.astype(BF16)
        h_ref[...] = hb
        os_ref[...] = jnp.dot(hb, ws_ref[...], preferred_element_type=F32)

    om_ref[...] = jnp.dot(h_ref[...], wm_ref[...], preferred_element_type=F32).astype(BF16)


def _inproj(x, g, w_main, w_small, tm=1024, tn=HEAD_W):
    t, d = x.shape
    grid = (t // tm, MAIN_W // tn)
    return pl.pallas_call(
        _inproj_kernel,
        grid=grid,
        in_specs=[
            pl.BlockSpec((tm, d), lambda i, j: (i, 0)),
            pl.BlockSpec((1, d), lambda i, j: (0, 0)),
            pl.BlockSpec((d, tn), lambda i, j: (0, j)),
            pl.BlockSpec((d, LANES), lambda i, j: (0, 0)),
        ],
        out_specs=[
            pl.BlockSpec((tm, tn), lambda i, j: (i, j)),
            pl.BlockSpec((tm, LANES), lambda i, j: (i, 0)),
        ],
        out_shape=[jax.ShapeDtypeStruct((t, MAIN_W), BF16),
                   jax.ShapeDtypeStruct((t, LANES), F32)],
        scratch_shapes=[pltpu.VMEM((tm, d), BF16)],
        compiler_params=_cparams(("parallel", "arbitrary")),
        name="inproj",
    )(x, g, w_main, w_small)


def _gla_kernel(q_ref, k_ref, v_ref, r_ref, sm_ref, wgu_ref, bg_ref, ng_ref, sel_ref, msk_ref,
                o_ref, st_ref, e_ref, *, n_chunks, group):
    @pl.when(pl.program_id(1) == 0)
    def _():
        st_ref[...] = jnp.zeros_like(st_ref)

    n_lv = len(_level_sizes())
    sel = sel_ref[...]
    eye2 = msk_ref[0] - msk_ref[1]
    z = _dot3(sm_ref[...], wgu_ref[...]) + bg_ref[...]
    la_all = (jnp.minimum(z, 0.0) - jnp.log1p(jnp.exp(-jnp.abs(z)))) * (1.0 / GLA_GATE_NORM)
    ng = ng_ref[...]
    zero_b = jnp.zeros((CHUNK, GLA_DK), BF16)

    def head_blocks(x2):
        return jnp.concatenate([jnp.concatenate([x2[:, :GLA_DK], zero_b], axis=1),
                                jnp.concatenate([zero_b, x2[:, GLA_DK:]], axis=1)], axis=0)

    sv, upd, qe, e_last = {}, {}, {}, {}
    for c0 in range(0, n_chunks, group):
        chunks = range(c0, min(c0 + group, n_chunks))
        for c in chunks:
            e_ref[c] = jnp.exp(_dot_sel(sel, la_all[c * CHUNK:(c + 1) * CHUNK]))
        units = [(c, hp) for c in chunks for hp in range(GLA_HEADS // 2)]
        lhs, rhs, vb, kd = {}, {}, {}, {}
        for (c, hp) in units:
            rows = slice(c * CHUNK, (c + 1) * CHUNK)
            cols2 = slice(hp * 2 * GLA_DK, (hp + 1) * 2 * GLA_DK)
            q2 = q_ref[rows, cols2].astype(F32) * (GLA_DK ** -0.5)
            k2 = k_ref[rows, cols2].astype(F32)
            ls, rs = [q2.astype(BF16)], [head_blocks(k2.astype(BF16))]
            for li in range(n_lv):
                e_l = e_ref[c, (2 + li) * CHUNK:(3 + li) * CHUNK, cols2]
                ls.append((q2 * e_l).astype(BF16))
                rs.append(head_blocks((k2 * e_l).astype(BF16)))
            lhs[(c, hp)], rhs[(c, hp)] = ls, rs
            e_q = e_ref[c, 0:CHUNK, cols2]
            qe[(c, hp)] = (q2 * e_q).astype(BF16)
            e_last[(c, hp)] = e_q[CHUNK - 1:CHUNK, :]
            kd[(c, hp)] = (k2 * e_ref[c, CHUNK:2 * CHUNK, cols2]).astype(BF16)
            vb[(c, hp)] = v_ref[rows, cols2]
        raw = {u: [_dot_nt(a, b_) for a, b_ in zip(lhs[u], rhs[u])] for u in units}
        for u in units:
            upd[u] = [_dot_tn(vb[u][:, h * GLA_DK:(h + 1) * GLA_DK], kd[u][:, h * GLA_DK:(h + 1) * GLA_DK])
                      for h in range(2)]
        for u in units:
            sc = eye2 * raw[u][0]
            for li in range(n_lv):
                sc = sc + msk_ref[2 + li] * raw[u][1 + li]
            sv[u] = jnp.dot(sc.astype(BF16), head_blocks(vb[u]), preferred_element_type=F32)

    for c in range(n_chunks):
        rows = slice(c * CHUNK, (c + 1) * CHUNK)
        for hp in range(GLA_HEADS // 2):
            u = (c, hp)
            for hh in range(2):
                h = 2 * hp + hh
                cols = slice(h * GLA_DK, (h + 1) * GLA_DK)
                lc = slice(hh * GLA_DK, (hh + 1) * GLA_DK)
                st = st_ref[h]
                o = _dot_nt(qe[u][:, lc], st) + sv[u][:, lc]
                st_ref[h] = st * e_last[u][:, lc] + upd[u][hh]
                on = o * lax.rsqrt(jnp.mean(o * o, axis=-1, keepdims=True) + RMS_EPS) * ng
                o_ref[rows, cols] = (on * _silu(r_ref[rows, cols].astype(F32))).astype(BF16)


def _gla(main, small, wgu_pad, b_gate, norm_g, b, s, ts=256, group=2):
    t = main.shape[0]
    nst = s // ts
    row = lambda bi, si: bi * nst + si
    sel = jnp.asarray(_np_gla_sel(), BF16)
    msk = jnp.asarray(np.tile(_np_masks(), (1, 1, 2)), F32)

    def col(cb):
        return pl.BlockSpec((ts, HEAD_W), lambda bi, si: (row(bi, si), cb))

    const2 = lambda bi, si: (0, 0)
    return pl.pallas_call(
        functools.partial(_gla_kernel, n_chunks=ts // CHUNK, group=group),
        grid=(b, nst),
        in_specs=[
            col(CB_GLA_Q), col(CB_GLA_K), col(CB_GLA_V), col(CB_GLA_R),
            pl.BlockSpec((ts, LANES), lambda bi, si: (row(bi, si), 0)),
            pl.BlockSpec(wgu_pad.shape, const2),
            pl.BlockSpec(b_gate.shape, const2),
            pl.BlockSpec(norm_g.shape, const2),
            pl.BlockSpec(sel.shape, const2),
            pl.BlockSpec(msk.shape, lambda bi, si: (0, 0, 0)),
        ],
        out_specs=pl.BlockSpec((ts, HEAD_W), lambda bi, si: (row(bi, si), 0)),
        out_shape=jax.ShapeDtypeStruct((t, HEAD_W), BF16),
        scratch_shapes=[pltpu.VMEM((GLA_HEADS, GLA_DK, GLA_DK), F32),
                        pltpu.VMEM((ts // CHUNK, sel.shape[0], HEAD_W), F32)],
        compiler_params=_cparams(("parallel", "arbitrary")),
        name="gla",
    )(main, main, main, main, small, wgu_pad, b_gate, norm_g, sel, msk)


def _att_kernel(q_ref, kp_ref, kc_ref, vp_ref, vc_ref, bias_ref, o_ref, kk_ref, vv_ref,
                *, n_chunks, left):
    band = left + CHUNK
    kk_ref[0:left, :] = kp_ref[...]
    kk_ref[left:2 * left, :] = kc_ref[...]
    vv_ref[0:left, :] = vp_ref[...]
    vv_ref[left:2 * left, :] = vc_ref[...]
    first = pl.program_id(1) == 0
    lane = lax.broadcasted_iota(jnp.int32, (CHUNK, LANES), 1)
    low_half = lane < ATT_HD

    def chunk_body(c, carry, *, masked):
        r0 = pl.multiple_of(c * CHUNK, CHUNK)
        scores = []
        for hp in range(ATT_HEADS // 2):
            cols = slice(hp * LANES, (hp + 1) * LANES)
            q2 = q_ref[pl.ds(r0, CHUNK), cols].astype(F32) * (ATT_HD ** -0.5)
            k2 = kk_ref[pl.ds(r0, band), cols]
            for par in range(2):
                qm = jnp.where(low_half if par == 0 else ~low_half, q2, 0.0).astype(BF16)
                scores.append(_dot_nt(qm, k2))
        if masked:
            jband = lax.broadcasted_iota(jnp.int32, (CHUNK, band), 1)
            valid = jband >= left - r0
        probs = []
        for h in range(ATT_HEADS):
            sc = scores[h] + bias_ref[h]
            if masked:
                sc = jnp.where(valid, sc, -jnp.inf)
            m = jnp.max(sc, axis=-1, keepdims=True)
            p = jnp.exp(sc - m)
            probs.append((p.astype(BF16), 1.0 / jnp.sum(p, axis=-1, keepdims=True)))
        for hp in range(ATT_HEADS // 2):
            cols = slice(hp * LANES, (hp + 1) * LANES)
            v2 = vv_ref[pl.ds(r0, band), cols]
            outs = [jnp.dot(probs[2 * hp + par][0], v2, preferred_element_type=F32) * probs[2 * hp + par][1]
                    for par in range(2)]
            o_ref[pl.ds(r0, CHUNK), cols] = jnp.where(low_half, outs[0], outs[1]).astype(BF16)
        return carry

    @pl.when(first)
    def _():
        lax.fori_loop(0, n_chunks, functools.partial(chunk_body, masked=True), 0)

    @pl.when(jnp.logical_not(first))
    def _():
        lax.fori_loop(0, n_chunks, functools.partial(chunk_body, masked=False), 0)


def _att(main, bias, b, s):
    t = main.shape[0]
    left = ATT_LEFT_CHUNKS * CHUNK
    tq = left
    nst = s // tq
    cur = lambda cb: pl.BlockSpec((tq, HEAD_W), lambda bi, si: (bi * nst + si, cb))
    prev = lambda cb: pl.BlockSpec((tq, HEAD_W),
                                   lambda bi, si: (bi * nst + jnp.maximum(si - 1, 0), cb))
    return pl.pallas_call(
        functools.partial(_att_kernel, n_chunks=tq // CHUNK, left=left),
        grid=(b, nst),
        in_specs=[
            cur(CB_ATT_Q), prev(CB_ATT_K), cur(CB_ATT_K), prev(CB_ATT_V), cur(CB_ATT_V),
            pl.BlockSpec(bias.shape, lambda bi, si: (0, 0, 0)),
        ],
        out_specs=pl.BlockSpec((tq, HEAD_W), lambda bi, si: (bi * nst + si, 0)),
        out_shape=jax.ShapeDtypeStruct((t, HEAD_W), BF16),
        scratch_shapes=[pltpu.VMEM((2 * left, HEAD_W), BF16),
                        pltpu.VMEM((2 * left, HEAD_W), BF16)],
        compiler_params=_cparams(("parallel", "arbitrary")),
        name="band_att",
    )(main, main, main, main, main, bias)


def _gdn_kernel(q_ref, k_ref, v_ref, z_ref, sm_ref, cw_ref, alog_ref, dtb_ref, ng_ref, msk_ref,
                o_ref, st_ref, xb_ref, qkv_ref, *, n_chunks, group):
    ts = n_chunks * CHUNK
    pad = 8

    @pl.when(pl.program_id(1) == 0)
    def _():
        st_ref[...] = jnp.zeros_like(st_ref)
        xb_ref[0:pad, :] = jnp.zeros((pad, 3 * HEAD_W), F32)

    xb_ref[pad:pad + ts, 0:HEAD_W] = q_ref[...].astype(F32)
    xb_ref[pad:pad + ts, HEAD_W:2 * HEAD_W] = k_ref[...].astype(F32)
    xb_ref[pad:pad + ts, 2 * HEAD_W:3 * HEAD_W] = v_ref[...].astype(F32)
    acc = None
    for j in range(CONV_W):
        off = pad - (CONV_W - 1) + j
        term = xb_ref[off:off + ts, :] * cw_ref[j:j + 1, :]
        acc = term if acc is None else acc + term
    xb_ref[0:pad, :] = xb_ref[ts:ts + pad, :]
    qkv_ref[...] = _silu(acc)

    sm = sm_ref[...]
    la_all = -jnp.exp(alog_ref[...]) * _softplus(sm + dtb_ref[...])
    beta_all = _sigmoid(sm)
    incl = msk_ref[0]
    strict = msk_ref[1]
    tril_b = incl.astype(BF16)
    eye = incl - strict
    ng = ng_ref[...]
    n_lv = len(_level_sizes())

    ops = {}
    for c0 in range(0, n_chunks, group):
        chunks = range(c0, min(c0 + group, n_chunks))
        pairs = [(c, h) for c in chunks for h in range(GDN_HEADS)]
        gc_alls = {c: _dot_sel(tril_b, la_all[c * CHUNK:(c + 1) * CHUNK]) for c in chunks}
        pre = {}
        for (c, h) in pairs:
            rows = slice(c * CHUNK, (c + 1) * CHUNK)
            la_col = la_all[rows, SMALL_A + h:SMALL_A + h + 1]
            beta = beta_all[rows, SMALL_B + h:SMALL_B + h + 1]
            gc = gc_alls[c][:, SMALL_A + h:SMALL_A + h + 1]
            gc_last = gc[CHUNK - 1:CHUNK, :]
            q = qkv_ref[rows, h * GDN_DK:(h + 1) * GDN_DK]
            k = qkv_ref[rows, HEAD_W + h * GDN_DK:HEAD_W + (h + 1) * GDN_DK]
            v = qkv_ref[rows, 2 * HEAD_W + h * GDN_DK:2 * HEAD_W + (h + 1) * GDN_DK]
            qn = q * lax.rsqrt(jnp.sum(q * q, axis=-1, keepdims=True) + RMS_EPS) * (GDN_DK ** -0.5)
            kn = k * lax.rsqrt(jnp.sum(k * k, axis=-1, keepdims=True) + RMS_EPS)
            kb = kn * beta
            pre[(c, h)] = dict(
                lcs=la_col * strict, qn=qn.astype(BF16), kn=kn.astype(BF16), kb=kb.astype(BF16),
                rhs=jnp.concatenate([v * beta, kb * jnp.exp(gc)], axis=1).astype(BF16),
                qg=qn * jnp.exp(gc), kd=(kn * jnp.exp(gc_last - gc)).astype(BF16),
                g_last=jnp.exp(gc_last))
        mm = {p: (_dot_sel(tril_b, pre[p]["lcs"]), _dot_nt(pre[p]["kb"], pre[p]["kn"]),
                  _dot_nt(pre[p]["qn"], pre[p]["kn"])) for p in pairs}
        a_mat, attn, p_inv = {}, {}, {}
        for p in pairs:
            dmat, kk, qk = mm[p]
            gamma = jnp.exp(dmat)
            a_mat[p] = strict * (kk * gamma)
            attn[p] = (incl * (qk * gamma)).astype(BF16)
            p_inv[p] = eye - a_mat[p] * msk_ref[2 + n_lv - 1]
        for li in range(n_lv - 2, -1, -1):
            half = {p: _dot(p_inv[p], a_mat[p] * msk_ref[2 + li]) for p in pairs}
            p_inv = {p: p_inv[p] - _dot(half[p], p_inv[p]) for p in pairs}
        sol = {p: _dot(p_inv[p], pre[p]["rhs"]).astype(BF16) for p in pairs}
        for p in pairs:
            nw = _dot_tn(pre[p]["kd"], sol[p])
            ao = _dot(attn[p], sol[p])
            ops[p] = (nw[:, :GDN_DK], nw[:, GDN_DK:].astype(BF16), ao[:, :GDN_DK],
                      (pre[p]["qg"] - ao[:, GDN_DK:]).astype(BF16), pre[p]["g_last"])

    for c in range(n_chunks):
        rows = slice(c * CHUNK, (c + 1) * CHUNK)
        res = {}
        for h in range(GDN_HEADS):
            n_mat, w2, o2, q2, g_last = ops[(c, h)]
            res[h] = jnp.dot(jnp.concatenate([w2, q2], axis=0), st_ref[h].astype(BF16),
                             preferred_element_type=F32)
        for h in range(GDN_HEADS):
            cols = slice(h * GDN_DK, (h + 1) * GDN_DK)
            n_mat, w2, o2, q2, g_last = ops[(c, h)]
            st_ref[h] = st_ref[h] * g_last - res[h][:GDN_DK] + n_mat
            o = res[h][GDN_DK:] + o2
            on = o * lax.rsqrt(jnp.mean(o * o, axis=-1, keepdims=True) + RMS_EPS) * ng
            o_ref[rows, cols] = (on * _silu(z_ref[rows, cols].astype(F32))).astype(BF16)


def _gdn(main, small, conv_w, alog_vec, dtb_vec, norm_g, b, s, ts=256, group=2):
    t = main.shape[0]
    nst = s // ts
    row = lambda bi, si: bi * nst + si
    msk = jnp.asarray(_np_masks(), F32)

    def col(cb):
        return pl.BlockSpec((ts, HEAD_W), lambda bi, si: (row(bi, si), cb))

    const2 = lambda bi, si: (0, 0)
    return pl.pallas_call(
        functools.partial(_gdn_kernel, n_chunks=ts // CHUNK, group=group),
        grid=(b, nst),
        in_specs=[
            col(CB_GDN_Q), col(CB_GDN_K), col(CB_GDN_V), col(CB_GDN_Z),
            pl.BlockSpec((ts, LANES), lambda bi, si: (row(bi, si), 0)),
            pl.BlockSpec(conv_w.shape, const2),
            pl.BlockSpec(alog_vec.shape, const2),
            pl.BlockSpec(dtb_vec.shape, const2),
            pl.BlockSpec(norm_g.shape, const2),
            pl.BlockSpec(msk.shape, lambda bi, si: (0, 0, 0)),
        ],
        out_specs=pl.BlockSpec((ts, HEAD_W), lambda bi, si: (row(bi, si), 0)),
        out_shape=jax.ShapeDtypeStruct((t, HEAD_W), BF16),
        scratch_shapes=[pltpu.VMEM((GDN_HEADS, GDN_DK, GDN_DK), F32),
                        pltpu.VMEM((ts + 8, 3 * HEAD_W), F32),
                        pltpu.VMEM((ts, 3 * HEAD_W), F32)],
        compiler_params=_cparams(("parallel", "arbitrary")),
        name="gdn",
    )(main, main, main, main, small, conv_w, alog_vec, dtb_vec, norm_g, msk)


def _merge_kernel(x_ref, m0_ref, m1_ref, m2_ref, og_ref, oa_ref, od_ref, wg_ref, wa_ref, wd_ref,
                  wo_ref, g2_ref, x1_ref, h2_ref):
    y = (_sigmoid(m0_ref[...].astype(F32)) * jnp.dot(og_ref[...], wg_ref[...], preferred_element_type=F32)
         + _sigmoid(m1_ref[...].astype(F32)) * jnp.dot(oa_ref[...], wa_ref[...], preferred_element_type=F32)
         + _sigmoid(m2_ref[...].astype(F32)) * jnp.dot(od_ref[...], wd_ref[...], preferred_element_type=F32))
    x1 = x_ref[...] + jnp.dot(y.astype(BF16), wo_ref[...], preferred_element_type=F32)
    x1_ref[...] = x1
    h2 = x1 * lax.rsqrt(jnp.mean(x1 * x1, axis=-1, keepdims=True) + RMS_EPS) * g2_ref[...]
    h2_ref[...] = h2.astype(BF16)


def _merge(x, main, o_gla, o_att, o_gdn, wg, wa, wd, wo, g2, tm=256):
    t, d = x.shape
    rowb = lambda i: (i, 0)
    const = lambda i: (0, 0)
    return pl.pallas_call(
        _merge_kernel,
        grid=(t // tm,),
        in_specs=[
            pl.BlockSpec((tm, d), rowb),
            pl.BlockSpec((tm, d), lambda i: (i, 0)),
            pl.BlockSpec((tm, d), lambda i: (i, 1)),
            pl.BlockSpec((tm, d), lambda i: (i, 2)),
            pl.BlockSpec((tm, HEAD_W), rowb),
            pl.BlockSpec((tm, HEAD_W), rowb),
            pl.BlockSpec((tm, HEAD_W), rowb),
            pl.BlockSpec(wg.shape, const),
            pl.BlockSpec(wa.shape, const),
            pl.BlockSpec(wd.shape, const),
            pl.BlockSpec(wo.shape, const),
            pl.BlockSpec(g2.shape, const),
        ],
        out_specs=[pl.BlockSpec((tm, d), rowb), pl.BlockSpec((tm, d), rowb)],
        out_shape=[jax.ShapeDtypeStruct((t, d), F32), jax.ShapeDtypeStruct((t, d), BF16)],
        compiler_params=_cparams(("parallel",)),
        name="merge_out",
    )(x, main, main, main, o_gla, o_att, o_gdn, wg, wa, wd, wo, g2)


def _ffn_kernel(x1_ref, h2_ref, wgate_ref, wup_ref, wdown_ref, gf_ref, o_ref, acc_ref, *, final_norm):
    j = pl.program_id(1)

    @pl.when(j == 0)
    def _():
        acc_ref[...] = x1_ref[...]

    h2 = h2_ref[...]
    gate = jnp.dot(h2, wgate_ref[...], preferred_element_type=F32)
    up = jnp.dot(h2, wup_ref[...], preferred_element_type=F32)
    acc_ref[...] += jnp.dot((_silu(gate) * up).astype(BF16), wdown_ref[...],
                            preferred_element_type=F32)

    @pl.when(j == pl.num_programs(1) - 1)
    def _():
        x2 = acc_ref[...]
        if final_norm:
            x2 = x2 * lax.rsqrt(jnp.mean(x2 * x2, axis=-1, keepdims=True) + RMS_EPS) * gf_ref[...]
        o_ref[...] = x2


def _ffn(x1, h2, w_in, w_out, gf, final_norm, tm=512, n_h=2):
    t, d = x1.shape
    hidden = w_out.shape[0]
    th = hidden // n_h
    return pl.pallas_call(
        functools.partial(_ffn_kernel, final_norm=final_norm),
        grid=(t // tm, n_h),
        in_specs=[
            pl.BlockSpec((tm, d), lambda i, j: (i, 0)),
            pl.BlockSpec((tm, d), lambda i, j: (i, 0)),
            pl.BlockSpec((d, th), lambda i, j: (0, j)),
            pl.BlockSpec((d, th), lambda i, j: (0, j + n_h)),
            pl.BlockSpec((th, d), lambda i, j: (j, 0)),
            pl.BlockSpec(gf.shape, lambda i, j: (0, 0)),
        ],
        out_specs=pl.BlockSpec((tm, d), lambda i, j: (i, 0)),
        out_shape=jax.ShapeDtypeStruct((t, d), F32),
        scratch_shapes=[pltpu.VMEM((tm, d), F32)],
        compiler_params=_cparams(("parallel", "arbitrary")),
        name="ffn",
    )(x1, h2, w_in, w_in, w_out, gf)


def _split_w_in(w_in):
    widths = (512, 512, 512, 512, GLA_GATE_RANK, 512, 512, 512, 512, 512, 512, 512,
              GDN_HEADS, GDN_HEADS, w_in.shape[1] - (11 * 512 + GLA_GATE_RANK + 2 * GDN_HEADS))
    bounds = np.cumsum(widths)[:-1]
    (gq, gk, gv, gr, glr, aq, ak, av, dq, dk, dv, dz, da, db, mg) = jnp.split(w_in, bounds, axis=1)
    main = jnp.concatenate([mg, gq, gk, gv, gr, aq, ak, av, dq, dk, dv, dz], axis=1)
    small = jnp.concatenate([glr, da, db], axis=1)
    small = jnp.pad(small, ((0, 0), (0, LANES - small.shape[1])))
    return main.astype(BF16), small.astype(BF16)


def _band_bias(rel_bias):
    left = ATT_LEFT_CHUNKS * CHUNK
    band = left + CHUNK
    rel = np.arange(CHUNK)[:, None] + left - np.arange(band)[None, :]
    idx = np.clip(rel, -(CHUNK - 1), REL_MAX) + (CHUNK - 1)
    return rel_bias[:, idx].astype(F32)


def _lane_vec(vals, offset):
    out = jnp.zeros((1, LANES), F32)
    return lax.dynamic_update_slice(out, vals.reshape(1, -1).astype(F32), (0, offset))


def kernel(x, mix_norm_g, w_in, gla_w_gate_up, gla_b_gate, gla_norm_g, att_rel_bias, gdn_conv_w,
           gdn_a_log, gdn_dt_bias, gdn_norm_g, w_branch_gla, w_branch_att, w_branch_gdn, w_out,
           ffn_norm_g, w_ffn_in, w_ffn_out, final_norm_g):
    b, s, d = x.shape
    depth = w_in.shape[0]
    xt = x.reshape(b * s, d)
    for l in range(depth):
        w_main, w_small = _split_w_in(w_in[l])
        main, small = _inproj(xt, mix_norm_g[l].reshape(1, d), w_main, w_small)
        wgu_pad = jnp.pad(gla_w_gate_up[l], ((0, LANES - GLA_GATE_RANK), (0, 0)))
        o_gla = _gla(main, small, wgu_pad, gla_b_gate[l].reshape(1, -1),
                     gla_norm_g[l].reshape(1, -1), b, s)
        o_att = _att(main, _band_bias(att_rel_bias[l]), b, s)
        o_gdn = _gdn(main, small, gdn_conv_w[l], _lane_vec(gdn_a_log[l], SMALL_A),
                     _lane_vec(gdn_dt_bias[l], SMALL_A), gdn_norm_g[l].reshape(1, -1), b, s)
        x1, h2 = _merge(xt, main, o_gla, o_att, o_gdn, w_branch_gla[l].astype(BF16),
                        w_branch_att[l].astype(BF16), w_branch_gdn[l].astype(BF16),
                        w_out[l].astype(BF16), ffn_norm_g[l].reshape(1, d))
        xt = _ffn(x1, h2, w_ffn_in[l].astype(BF16), w_ffn_out[l].astype(BF16),
                  final_norm_g.reshape(1, d), final_norm=(l == depth - 1))
    return xt.reshape(b, s, d)
```

```python
import functools

import numpy as np
import jax
import jax.numpy as jnp
from jax import lax
from jax.experimental import pallas as pl
from jax.experimental.pallas import tpu as pltpu

F32 = jnp.float32
BF16 = jnp.bfloat16

CHUNK = 64
RMS_EPS = 1e-6
GLA_HEADS = 4
GLA_DK = 128
GLA_GATE_RANK = 16
GLA_GATE_NORM = 16.0
ATT_HEADS = 8
ATT_HD = 64
ATT_LEFT_CHUNKS = 8
REL_MAX = 256
GDN_HEADS = 4
GDN_DK = 128
CONV_W = 4
HEAD_W = 512
LANES = 128
SMALL_A = GLA_GATE_RANK
SMALL_B = GLA_GATE_RANK + GDN_HEADS

CB_GLA_Q, CB_GLA_K, CB_GLA_V, CB_GLA_R = 6, 7, 8, 9
CB_ATT_Q, CB_ATT_K, CB_ATT_V = 10, 11, 12
CB_GDN_Q, CB_GDN_K, CB_GDN_V, CB_GDN_Z = 13, 14, 15, 16
MAIN_W = 17 * HEAD_W

VMEM_LIMIT = 56 * 1024 * 1024


def _cparams(sem):
    return pltpu.CompilerParams(dimension_semantics=sem, vmem_limit_bytes=VMEM_LIMIT)


def _dot(a, b):
    return jnp.dot(a.astype(BF16), b.astype(BF16), preferred_element_type=F32)


def _dot_nt(a, b):
    return lax.dot_general(a.astype(BF16), b.astype(BF16), (((1,), (1,)), ((), ())),
                           preferred_element_type=F32)


def _dot_tn(a, b):
    return lax.dot_general(a.astype(BF16), b.astype(BF16), (((0,), (0,)), ((), ())),
                           preferred_element_type=F32)


def _split(x):
    hi = x.astype(BF16)
    lo = (x - hi.astype(F32)).astype(BF16)
    return hi, lo


def _dot_sel(m_bf16, x):
    hi, lo = _split(x)
    return (jnp.dot(m_bf16, hi, preferred_element_type=F32)
            + jnp.dot(m_bf16, lo, preferred_element_type=F32))


def _dot3(a, b):
    ah, al = _split(a)
    bh, bl = _split(b)
    return (jnp.dot(ah, bh, preferred_element_type=F32)
            + jnp.dot(al, bh, preferred_element_type=F32)
            + jnp.dot(ah, bl, preferred_element_type=F32))


def _sigmoid(x):
    return 0.5 * jnp.tanh(0.5 * x) + 0.5


def _silu(x):
    return x * _sigmoid(x)


def _softplus(x):
    return jnp.maximum(x, 0.0) + jnp.log1p(jnp.exp(-jnp.abs(x)))


def _level_sizes():
    s, out = CHUNK // 2, []
    while s >= 1:
        out.append(s)
        s //= 2
    return out


def _np_masks():
    i = np.arange(CHUNK)[:, None]
    j = np.arange(CHUNK)[None, :]
    ms = [(j <= i), (j < i)]
    for s in _level_sizes():
        bi, bj = i // s, j // s
        ms.append((bi % 2 == 1) & (bj == bi - 1))
    return np.stack(ms).astype(np.float32)


def _np_gla_sel():
    i = np.arange(CHUNK)[:, None]
    m = np.arange(CHUNK)[None, :]
    blocks = [(m <= i), (m > i)]
    for s in _level_sizes():
        p = i // s
        odd = (p % 2 == 1)
        b_odd = p * s - 1
        b_even = (p + 1) * s - 1
        blocks.append(np.where(odd, (m > b_odd) & (m <= i), (m > i) & (m <= b_even)))
    return np.concatenate(blocks, axis=0).astype(np.float32)


def _inproj_kernel(x_ref, g_ref, wm_ref, ws_ref, om_ref, os_ref, h_ref):
    @pl.when(pl.program_id(1) == 0)
    def _():
        x = x_ref[...]
        h = x * lax.rsqrt(jnp.mean(x * x, axis=-1, keepdims=True) + RMS_EPS) * g_ref[...]
        hb = h.astype(BF16)
        h_ref[...] = hb
        os_ref[...] = jnp.dot(hb, ws_ref[...], preferred_element_type=F32)

    om_ref[...] = jnp.dot(h_ref[...], wm_ref[...], preferred_element_type=F32).astype(BF16)


def _inproj(x, g, w_main, w_small, tm=1024, tn=MAIN_W // 4):
    t, d = x.shape
    grid = (t // tm, MAIN_W // tn)
    return pl.pallas_call(
        _inproj_kernel,
        grid=grid,
        in_specs=[
            pl.BlockSpec((tm, d), lambda i, j: (i, 0)),
            pl.BlockSpec((1, d), lambda i, j: (0, 0)),
            pl.BlockSpec((d, tn), lambda i, j: (0, j)),
            pl.BlockSpec((d, LANES), lambda i, j: (0, 0)),
        ],
        out_specs=[
            pl.BlockSpec((tm, tn), lambda i, j: (i, j)),
            pl.BlockSpec((tm, LANES), lambda i, j: (i, 0)),
        ],
        out_shape=[jax.ShapeDtypeStruct((t, MAIN_W), BF16),
                   jax.ShapeDtypeStruct((t, LANES), F32)],
        scratch_shapes=[pltpu.VMEM((tm, d), BF16)],
        compiler_params=_cparams(("parallel", "arbitrary")),
        name="inproj",
    )(x, g, w_main, w_small)


def _gla_kernel(q_ref, k_ref, v_ref, r_ref, sm_ref, wgu_ref, bg_ref, ng_ref, sel_ref, msk_ref,
                o_ref, st_ref, e_ref, *, n_chunks, group):
    @pl.when(pl.program_id(1) == 0)
    def _():
        st_ref[...] = jnp.zeros_like(st_ref)

    n_lv = len(_level_sizes())
    sel = sel_ref[...]
    eye2 = msk_ref[0] - msk_ref[1]
    z = _dot3(sm_ref[...], wgu_ref[...]) + bg_ref[...]
    la_all = (jnp.minimum(z, 0.0) - jnp.log1p(jnp.exp(-jnp.abs(z)))) * (1.0 / GLA_GATE_NORM)
    ng = ng_ref[...]
    zero_b = jnp.zeros((CHUNK, GLA_DK), BF16)

    def head_blocks(x2):
        return jnp.concatenate([jnp.concatenate([x2[:, :GLA_DK], zero_b], axis=1),
                                jnp.concatenate([zero_b, x2[:, GLA_DK:]], axis=1)], axis=0)

    sv, upd, qe, e_last = {}, {}, {}, {}
    for c0 in range(0, n_chunks, group):
        chunks = range(c0, min(c0 + group, n_chunks))
        for c in chunks:
            hi, lo = _split(la_all[c * CHUNK:(c + 1) * CHUNK])
            e_ref[c] = jnp.exp(jnp.dot(sel, jnp.concatenate([hi, lo], axis=0), preferred_element_type=F32))
        units = [(c, hp) for c in chunks for hp in range(GLA_HEADS // 2)]
        lhs, rhs, vb, kd = {}, {}, {}, {}
        for (c, hp) in units:
            rows = slice(c * CHUNK, (c + 1) * CHUNK)
            cols2 = slice(hp * 2 * GLA_DK, (hp + 1) * 2 * GLA_DK)
            q2 = q_ref[rows, cols2].astype(F32) * (GLA_DK ** -0.5)
            k2 = k_ref[rows, cols2].astype(F32)
            ls, rs = [q2.astype(BF16)], [head_blocks(k2.astype(BF16))]
            for li in range(n_lv):
                e_l = e_ref[c, (2 + li) * CHUNK:(3 + li) * CHUNK, cols2]
                ls.append((q2 * e_l).astype(BF16))
                rs.append(head_blocks((k2 * e_l).astype(BF16)))
            lhs[(c, hp)], rhs[(c, hp)] = ls, rs
            e_q = e_ref[c, 0:CHUNK, cols2]
            qe[(c, hp)] = (q2 * e_q).astype(BF16)
            e_last[(c, hp)] = e_q[CHUNK - 1:CHUNK, :]
            kd[(c, hp)] = (k2 * e_ref[c, CHUNK:2 * CHUNK, cols2]).astype(BF16)
            vb[(c, hp)] = v_ref[rows, cols2]
        raw = {u: [_dot_nt(a, b_) for a, b_ in zip(lhs[u], rhs[u])] for u in units}
        for u in units:
            upd[u] = [_dot_tn(vb[u][:, h * GLA_DK:(h + 1) * GLA_DK], kd[u][:, h * GLA_DK:(h + 1) * GLA_DK])
                      for h in range(2)]
        for u in units:
            sc = eye2 * raw[u][0]
            for li in range(n_lv):
                sc = sc + msk_ref[2 + li] * raw[u][1 + li]
            sv[u] = jnp.dot(sc.astype(BF16), head_blocks(vb[u]), preferred_element_type=F32)

    for c in range(n_chunks):
        rows = slice(c * CHUNK, (c + 1) * CHUNK)
        for hp in range(GLA_HEADS // 2):
            u = (c, hp)
            for hh in range(2):
                h = 2 * hp + hh
                cols = slice(h * GLA_DK, (h + 1) * GLA_DK)
                lc = slice(hh * GLA_DK, (hh + 1) * GLA_DK)
                st = st_ref[h]
                o = _dot_nt(qe[u][:, lc], st) + sv[u][:, lc]
                st_ref[h] = st * e_last[u][:, lc] + upd[u][hh]
                on = o * lax.rsqrt(jnp.mean(o * o, axis=-1, keepdims=True) + RMS_EPS) * ng
                o_ref[rows, cols] = (on * _silu(r_ref[rows, cols].astype(F32))).astype(BF16)


def _gla(main, small, wgu_pad, b_gate, norm_g, b, s, ts=256, group=2):
    t = main.shape[0]
    nst = s // ts
    row = lambda bi, si: bi * nst + si
    sel = jnp.asarray(np.tile(_np_gla_sel(), (1, 2)), BF16)
    msk = jnp.asarray(np.tile(_np_masks(), (1, 1, 2)), F32)

    def col(cb):
        return pl.BlockSpec((ts, HEAD_W), lambda bi, si: (row(bi, si), cb))

    const2 = lambda bi, si: (0, 0)
    return pl.pallas_call(
        functools.partial(_gla_kernel, n_chunks=ts // CHUNK, group=group),
        grid=(b, nst),
        in_specs=[
            col(CB_GLA_Q), col(CB_GLA_K), col(CB_GLA_V), col(CB_GLA_R),
            pl.BlockSpec((ts, LANES), lambda bi, si: (row(bi, si), 0)),
            pl.BlockSpec(wgu_pad.shape, const2),
            pl.BlockSpec(b_gate.shape, const2),
            pl.BlockSpec(norm_g.shape, const2),
            pl.BlockSpec(sel.shape, const2),
            pl.BlockSpec(msk.shape, lambda bi, si: (0, 0, 0)),
        ],
        out_specs=pl.BlockSpec((ts, HEAD_W), lambda bi, si: (row(bi, si), 0)),
        out_shape=jax.ShapeDtypeStruct((t, HEAD_W), BF16),
        scratch_shapes=[pltpu.VMEM((GLA_HEADS, GLA_DK, GLA_DK), F32),
                        pltpu.VMEM((ts // CHUNK, sel.shape[0], HEAD_W), F32)],
        compiler_params=_cparams(("parallel", "arbitrary")),
        name="gla",
    )(main, main, main, main, small, wgu_pad, b_gate, norm_g, sel, msk)


def _att_kernel(q_ref, kp_ref, kc_ref, vp_ref, vc_ref, bias_ref, o_ref, kk_ref, vv_ref,
                *, n_chunks, left):
    band = left + CHUNK
    kk_ref[0:left, :] = kp_ref[...]
    kk_ref[left:2 * left, :] = kc_ref[...]
    vv_ref[0:left, :] = vp_ref[...]
    vv_ref[left:2 * left, :] = vc_ref[...]
    first = pl.program_id(1) == 0
    lane = lax.broadcasted_iota(jnp.int32, (CHUNK, LANES), 1)
    low_half = lane < ATT_HD

    def chunk_body(c, carry, *, masked):
        r0 = pl.multiple_of(c * CHUNK, CHUNK)
        scores = []
        for hp in range(ATT_HEADS // 2):
            cols = slice(hp * LANES, (hp + 1) * LANES)
            q2 = q_ref[pl.ds(r0, CHUNK), cols].astype(F32) * (ATT_HD ** -0.5)
            k2 = kk_ref[pl.ds(r0, band), cols]
            for par in range(2):
                qm = jnp.where(low_half if par == 0 else ~low_half, q2, 0.0).astype(BF16)
                scores.append(_dot_nt(qm, k2))
        if masked:
            jband = lax.broadcasted_iota(jnp.int32, (CHUNK, band), 1)
            valid = jband >= left - r0
        probs = []
        for h in range(ATT_HEADS):
            sc = scores[h] + bias_ref[h]
            if masked:
                sc = jnp.where(valid, sc, -jnp.inf)
            m = jnp.max(sc, axis=-1, keepdims=True)
            p = jnp.exp(sc - m)
            probs.append((p.astype(BF16), 1.0 / jnp.sum(p, axis=-1, keepdims=True)))
        for hp in range(ATT_HEADS // 2):
            cols = slice(hp * LANES, (hp + 1) * LANES)
            v2 = vv_ref[pl.ds(r0, band), cols]
            outs = [jnp.dot(probs[2 * hp + par][0], v2, preferred_element_type=F32) * probs[2 * hp + par][1]
                    for par in range(2)]
            o_ref[pl.ds(r0, CHUNK), cols] = jnp.where(low_half, outs[0], outs[1]).astype(BF16)
        return carry

    @pl.when(first)
    def _():
        lax.fori_loop(0, n_chunks, functools.partial(chunk_body, masked=True), 0)

    @pl.when(jnp.logical_not(first))
    def _():
        lax.fori_loop(0, n_chunks, functools.partial(chunk_body, masked=False), 0)


def _att(main, bias, b, s):
    t = main.shape[0]
    left = ATT_LEFT_CHUNKS * CHUNK
    tq = left
    nst = s // tq
    cur = lambda cb: pl.BlockSpec((tq, HEAD_W), lambda bi, si: (bi * nst + si, cb))
    prev = lambda cb: pl.BlockSpec((tq, HEAD_W),
                                   lambda bi, si: (bi * nst + jnp.maximum(si - 1, 0), cb))
    return pl.pallas_call(
        functools.partial(_att_kernel, n_chunks=tq // CHUNK, left=left),
        grid=(b, nst),
        in_specs=[
            cur(CB_ATT_Q), prev(CB_ATT_K), cur(CB_ATT_K), prev(CB_ATT_V), cur(CB_ATT_V),
            pl.BlockSpec(bias.shape, lambda bi, si: (0, 0, 0)),
        ],
        out_specs=pl.BlockSpec((tq, HEAD_W), lambda bi, si: (bi * nst + si, 0)),
        out_shape=jax.ShapeDtypeStruct((t, HEAD_W), BF16),
        scratch_shapes=[pltpu.VMEM((2 * left, HEAD_W), BF16),
                        pltpu.VMEM((2 * left, HEAD_W), BF16)],
        compiler_params=_cparams(("parallel", "arbitrary")),
        name="band_att",
    )(main, main, main, main, main, bias)


def _gdn_kernel(q_ref, k_ref, v_ref, z_ref, sm_ref, cw_ref, alog_ref, dtb_ref, ng_ref, msk_ref,
                o_ref, st_ref, xb_ref, qkv_ref, *, n_chunks, group):
    ts = n_chunks * CHUNK
    pad = 8

    @pl.when(pl.program_id(1) == 0)
    def _():
        st_ref[...] = jnp.zeros_like(st_ref)
        xb_ref[0:pad, :] = jnp.zeros((pad, 3 * HEAD_W), F32)

    xb_ref[pad:pad + ts, 0:HEAD_W] = q_ref[...].astype(F32)
    xb_ref[pad:pad + ts, HEAD_W:2 * HEAD_W] = k_ref[...].astype(F32)
    xb_ref[pad:pad + ts, 2 * HEAD_W:3 * HEAD_W] = v_ref[...].astype(F32)
    acc = None
    for j in range(CONV_W):
        off = pad - (CONV_W - 1) + j
        term = xb_ref[off:off + ts, :] * cw_ref[j:j + 1, :]
        acc = term if acc is None else acc + term
    xb_ref[0:pad, :] = xb_ref[ts:ts + pad, :]
    qkv_ref[...] = _silu(acc)

    sm = sm_ref[...]
    la_all = -jnp.exp(alog_ref[...]) * _softplus(sm + dtb_ref[...])
    beta_all = _sigmoid(sm)
    incl = msk_ref[0]
    strict = msk_ref[1]
    tril_b = incl.astype(BF16)
    eye = incl - strict
    ng = ng_ref[...]
    n_lv = len(_level_sizes())

    ops = {}
    for c0 in range(0, n_chunks, group):
        chunks = range(c0, min(c0 + group, n_chunks))
        pairs = [(c, h) for c in chunks for h in range(GDN_HEADS)]
        gc_alls = {c: _dot_sel(tril_b, la_all[c * CHUNK:(c + 1) * CHUNK]) for c in chunks}
        pre = {}
        for (c, h) in pairs:
            rows = slice(c * CHUNK, (c + 1) * CHUNK)
            la_col = la_all[rows, SMALL_A + h:SMALL_A + h + 1]
            beta = beta_all[rows, SMALL_B + h:SMALL_B + h + 1]
            gc = gc_alls[c][:, SMALL_A + h:SMALL_A + h + 1]
            gc_last = gc[CHUNK - 1:CHUNK, :]
            q = qkv_ref[rows, h * GDN_DK:(h + 1) * GDN_DK]
            k = qkv_ref[rows, HEAD_W + h * GDN_DK:HEAD_W + (h + 1) * GDN_DK]
            v = qkv_ref[rows, 2 * HEAD_W + h * GDN_DK:2 * HEAD_W + (h + 1) * GDN_DK]
            qn = q * lax.rsqrt(jnp.sum(q * q, axis=-1, keepdims=True) + RMS_EPS) * (GDN_DK ** -0.5)
            kn = k * lax.rsqrt(jnp.sum(k * k, axis=-1, keepdims=True) + RMS_EPS)
            kb = kn * beta
            pre[(c, h)] = dict(
                lcs=la_col * strict, qn=qn.astype(BF16), kn=kn.astype(BF16), kb=kb.astype(BF16),
                rhs=jnp.concatenate([v * beta, kb * jnp.exp(gc)], axis=1).astype(BF16),
                qg=qn * jnp.exp(gc), kd=(kn * jnp.exp(gc_last - gc)).astype(BF16),
                g_last=jnp.exp(gc_last))
        mm = {p: (_dot_sel(tril_b, pre[p]["lcs"]), _dot_nt(pre[p]["kb"], pre[p]["kn"]),
                  _dot_nt(pre[p]["qn"], pre[p]["kn"])) for p in pairs}
        a_mat, attn, p_inv = {}, {}, {}
        for p in pairs:
            dmat, kk, qk = mm[p]
            gamma = jnp.exp(dmat)
            a_mat[p] = strict * (kk * gamma)
            attn[p] = (incl * (qk * gamma)).astype(BF16)
            p_inv[p] = eye - a_mat[p] * msk_ref[2 + n_lv - 1]
        for li in range(n_lv - 2, -1, -1):
            half = {p: _dot(p_inv[p], a_mat[p] * msk_ref[2 + li]) for p in pairs}
            p_inv = {p: p_inv[p] - _dot(half[p], p_inv[p]) for p in pairs}
        sol = {p: _dot(p_inv[p], pre[p]["rhs"]).astype(BF16) for p in pairs}
        for p in pairs:
            nw = _dot_tn(pre[p]["kd"], sol[p])
            ao = _dot(attn[p], sol[p])
            ops[p] = (nw[:, :GDN_DK], nw[:, GDN_DK:].astype(BF16), ao[:, :GDN_DK],
                      (pre[p]["qg"] - ao[:, GDN_DK:]).astype(BF16), pre[p]["g_last"])

    for c in range(n_chunks):
        rows = slice(c * CHUNK, (c + 1) * CHUNK)
        res = {}
        for h in range(GDN_HEADS):
            n_mat, w2, o2, q2, g_last = ops[(c, h)]
            res[h] = jnp.dot(jnp.concatenate([w2, q2], axis=0), st_ref[h].astype(BF16),
                             preferred_element_type=F32)
        for h in range(GDN_HEADS):
            cols = slice(h * GDN_DK, (h + 1) * GDN_DK)
            n_mat, w2, o2, q2, g_last = ops[(c, h)]
            st_ref[h] = st_ref[h] * g_last - res[h][:GDN_DK] + n_mat
            o = res[h][GDN_DK:] + o2
            on = o * lax.rsqrt(jnp.mean(o * o, axis=-1, keepdims=True) + RMS_EPS) * ng
            o_ref[rows, cols] = (on * _silu(z_ref[rows, cols].astype(F32))).astype(BF16)


def _gdn(main, small, conv_w, alog_vec, dtb_vec, norm_g, b, s, ts=256, group=4):
    t = main.shape[0]
    nst = s // ts
    row = lambda bi, si: bi * nst + si
    msk = jnp.asarray(_np_masks(), F32)

    def col(cb):
        return pl.BlockSpec((ts, HEAD_W), lambda bi, si: (row(bi, si), cb))

    const2 = lambda bi, si: (0, 0)
    return pl.pallas_call(
        functools.partial(_gdn_kernel, n_chunks=ts // CHUNK, group=group),
        grid=(b, nst),
        in_specs=[
            col(CB_GDN_Q), col(CB_GDN_K), col(CB_GDN_V), col(CB_GDN_Z),
            pl.BlockSpec((ts, LANES), lambda bi, si: (row(bi, si), 0)),
            pl.BlockSpec(conv_w.shape, const2),
            pl.BlockSpec(alog_vec.shape, const2),
            pl.BlockSpec(dtb_vec.shape, const2),
            pl.BlockSpec(norm_g.shape, const2),
            pl.BlockSpec(msk.shape, lambda bi, si: (0, 0, 0)),
        ],
        out_specs=pl.BlockSpec((ts, HEAD_W), lambda bi, si: (row(bi, si), 0)),
        out_shape=jax.ShapeDtypeStruct((t, HEAD_W), BF16),
        scratch_shapes=[pltpu.VMEM((GDN_HEADS, GDN_DK, GDN_DK), F32),
                        pltpu.VMEM((ts + 8, 3 * HEAD_W), F32),
                        pltpu.VMEM((ts, 3 * HEAD_W), F32)],
        compiler_params=_cparams(("parallel", "arbitrary")),
        name="gdn",
    )(main, main, main, main, small, conv_w, alog_vec, dtb_vec, norm_g, msk)


def _merge_kernel(x_ref, m0_ref, m1_ref, m2_ref, og_ref, oa_ref, od_ref, wg_ref, wa_ref, wd_ref,
                  wo_ref, g2_ref, x1_ref, h2_ref):
    y = (_sigmoid(m0_ref[...].astype(F32)) * jnp.dot(og_ref[...], wg_ref[...], preferred_element_type=F32)
         + _sigmoid(m1_ref[...].astype(F32)) * jnp.dot(oa_ref[...], wa_ref[...], preferred_element_type=F32)
         + _sigmoid(m2_ref[...].astype(F32)) * jnp.dot(od_ref[...], wd_ref[...], preferred_element_type=F32))
    x1 = x_ref[...] + jnp.dot(y.astype(BF16), wo_ref[...], preferred_element_type=F32)
    x1_ref[...] = x1
    h2 = x1 * lax.rsqrt(jnp.mean(x1 * x1, axis=-1, keepdims=True) + RMS_EPS) * g2_ref[...]
    h2_ref[...] = h2.astype(BF16)


def _merge(x, main, o_gla, o_att, o_gdn, wg, wa, wd, wo, g2, tm=512):
    t, d = x.shape
    rowb = lambda i: (i, 0)
    const = lambda i: (0, 0)
    return pl.pallas_call(
        _merge_kernel,
        grid=(t // tm,),
        in_specs=[
            pl.BlockSpec((tm, d), rowb),
            pl.BlockSpec((tm, d), lambda i: (i, 0)),
            pl.BlockSpec((tm, d), lambda i: (i, 1)),
            pl.BlockSpec((tm, d), lambda i: (i, 2)),
            pl.BlockSpec((tm, HEAD_W), rowb),
            pl.BlockSpec((tm, HEAD_W), rowb),
            pl.BlockSpec((tm, HEAD_W), rowb),
            pl.BlockSpec(wg.shape, const),
            pl.BlockSpec(wa.shape, const),
            pl.BlockSpec(wd.shape, const),
            pl.BlockSpec(wo.shape, const),
            pl.BlockSpec(g2.shape, const),
        ],
        out_specs=[pl.BlockSpec((tm, d), rowb), pl.BlockSpec((tm, d), rowb)],
        out_shape=[jax.ShapeDtypeStruct((t, d), F32), jax.ShapeDtypeStruct((t, d), BF16)],
        compiler_params=_cparams(("parallel",)),
        name="merge_out",
    )(x, main, main, main, o_gla, o_att, o_gdn, wg, wa, wd, wo, g2)


def _ffn_kernel(x1_ref, h2_ref, wgate_ref, wup_ref, wdown_ref, gf_ref, o_ref, acc_ref, *, final_norm):
    j = pl.program_id(1)

    @pl.when(j == 0)
    def _():
        acc_ref[...] = x1_ref[...]

    h2 = h2_ref[...]
    gate = jnp.dot(h2, wgate_ref[...], preferred_element_type=F32)
    up = jnp.dot(h2, wup_ref[...], preferred_element_type=F32)
    acc_ref[...] += jnp.dot((_silu(gate) * up).astype(BF16), wdown_ref[...],
                            preferred_element_type=F32)

    @pl.when(j == pl.num_programs(1) - 1)
    def _():
        x2 = acc_ref[...]
        if final_norm:
            x2 = x2 * lax.rsqrt(jnp.mean(x2 * x2, axis=-1, keepdims=True) + RMS_EPS) * gf_ref[...]
        o_ref[...] = x2


def _ffn(x1, h2, w_in, w_out, gf, final_norm, tm=512, n_h=2):
    t, d = x1.shape
    hidden = w_out.shape[0]
    th = hidden // n_h
    return pl.pallas_call(
        functools.partial(_ffn_kernel, final_norm=final_norm),
        grid=(t // tm, n_h),
        in_specs=[
            pl.BlockSpec((tm, d), lambda i, j: (i, 0)),
            pl.BlockSpec((tm, d), lambda i, j: (i, 0)),
            pl.BlockSpec((d, th), lambda i, j: (0, j)),
            pl.BlockSpec((d, th), lambda i, j: (0, j + n_h)),
            pl.BlockSpec((th, d), lambda i, j: (j, 0)),
            pl.BlockSpec(gf.shape, lambda i, j: (0, 0)),
        ],
        out_specs=pl.BlockSpec((tm, d), lambda i, j: (i, 0)),
        out_shape=jax.ShapeDtypeStruct((t, d), F32),
        scratch_shapes=[pltpu.VMEM((tm, d), F32)],
        compiler_params=_cparams(("parallel", "arbitrary")),
        name="ffn",
    )(x1, h2, w_in, w_in, w_out, gf)


def _split_w_in(w_in):
    widths = (512, 512, 512, 512, GLA_GATE_RANK, 512, 512, 512, 512, 512, 512, 512,
              GDN_HEADS, GDN_HEADS, w_in.shape[1] - (11 * 512 + GLA_GATE_RANK + 2 * GDN_HEADS))
    bounds = np.cumsum(widths)[:-1]
    (gq, gk, gv, gr, glr, aq, ak, av, dq, dk, dv, dz, da, db, mg) = jnp.split(w_in, bounds, axis=1)
    main = jnp.concatenate([mg, gq, gk, gv, gr, aq, ak, av, dq, dk, dv, dz], axis=1)
    small = jnp.concatenate([glr, da, db], axis=1)
    small = jnp.pad(small, ((0, 0), (0, LANES - small.shape[1])))
    return main.astype(BF16), small.astype(BF16)


def _band_bias(rel_bias):
    left = ATT_LEFT_CHUNKS * CHUNK
    band = left + CHUNK
    n_ext = left + 2 * CHUNK - 1
    ext = jnp.concatenate([rel_bias, jnp.repeat(rel_bias[:, -1:], n_ext - rel_bias.shape[1], axis=1)],
                          axis=1)[:, ::-1]
    rows = [ext[:, CHUNK - 1 - i:CHUNK - 1 - i + band] for i in range(CHUNK)]
    return jnp.stack(rows, axis=1).astype(F32)


def _lane_vec(vals, offset):
    out = jnp.zeros((1, LANES), F32)
    return lax.dynamic_update_slice(out, vals.reshape(1, -1).astype(F32), (0, offset))


def kernel(x, mix_norm_g, w_in, gla_w_gate_up, gla_b_gate, gla_norm_g, att_rel_bias, gdn_conv_w,
           gdn_a_log, gdn_dt_bias, gdn_norm_g, w_branch_gla, w_branch_att, w_branch_gdn, w_out,
           ffn_norm_g, w_ffn_in, w_ffn_out, final_norm_g):
    b, s, d = x.shape
    depth = w_in.shape[0]
    xt = x.reshape(b * s, d)
    for l in range(depth):
        w_main, w_small = _split_w_in(w_in[l])
        main, small = _inproj(xt, mix_norm_g[l].reshape(1, d), w_main, w_small)
        wgu_pad = jnp.pad(gla_w_gate_up[l], ((0, LANES - GLA_GATE_RANK), (0, 0)))
        o_gla = _gla(main, small, wgu_pad, gla_b_gate[l].reshape(1, -1),
                     gla_norm_g[l].reshape(1, -1), b, s)
        o_att = _att(main, _band_bias(att_rel_bias[l]), b, s)
        o_gdn = _gdn(main, small, gdn_conv_w[l], _lane_vec(gdn_a_log[l], SMALL_A),
                     _lane_vec(gdn_dt_bias[l], SMALL_A), gdn_norm_g[l].reshape(1, -1), b, s)
        x1, h2 = _merge(xt, main, o_gla, o_att, o_gdn, w_branch_gla[l].astype(BF16),
                        w_branch_att[l].astype(BF16), w_branch_gdn[l].astype(BF16),
                        w_out[l].astype(BF16), ffn_norm_g[l].reshape(1, d))
        xt = _ffn(x1, h2, w_ffn_in[l].astype(BF16), w_ffn_out[l].astype(BF16),
                  final_norm_g.reshape(1, d), final_norm=(l == depth - 1))
    return xt.reshape(b, s, d)
```

```python
import functools

import numpy as np
import jax
import jax.numpy as jnp
from jax import lax
from jax.experimental import pallas as pl
from jax.experimental.pallas import tpu as pltpu

F32 = jnp.float32
BF16 = jnp.bfloat16

CHUNK = 64
RMS_EPS = 1e-6
GLA_HEADS = 4
GLA_DK = 128
GLA_GATE_RANK = 16
GLA_GATE_NORM = 16.0
ATT_HEADS = 8
ATT_HD = 64
ATT_LEFT_CHUNKS = 8
REL_MAX = 256
GDN_HEADS = 4
GDN_DK = 128
CONV_W = 4
HEAD_W = 512
LANES = 128
SMALL_A = GLA_GATE_RANK
SMALL_B = GLA_GATE_RANK + GDN_HEADS

CB_GLA_Q, CB_GLA_K, CB_GLA_V, CB_GLA_R = 6, 7, 8, 9
CB_ATT_Q, CB_ATT_K, CB_ATT_V = 10, 11, 12
CB_GDN_Q, CB_GDN_K, CB_GDN_V, CB_GDN_Z = 13, 14, 15, 16
MAIN_W = 17 * HEAD_W

VMEM_LIMIT = 56 * 1024 * 1024


def _cparams(sem):
    return pltpu.CompilerParams(dimension_semantics=sem, vmem_limit_bytes=VMEM_LIMIT)


def _dot(a, b):
    return jnp.dot(a.astype(BF16), b.astype(BF16), preferred_element_type=F32)


def _dot_nt(a, b):
    return lax.dot_general(a.astype(BF16), b.astype(BF16), (((1,), (1,)), ((), ())),
                           preferred_element_type=F32)


def _dot_tn(a, b):
    return lax.dot_general(a.astype(BF16), b.astype(BF16), (((0,), (0,)), ((), ())),
                           preferred_element_type=F32)


def _split(x):
    hi = x.astype(BF16)
    lo = (x - hi.astype(F32)).astype(BF16)
    return hi, lo


def _dot_sel(m_bf16, x):
    hi, lo = _split(x)
    return (jnp.dot(m_bf16, hi, preferred_element_type=F32)
            + jnp.dot(m_bf16, lo, preferred_element_type=F32))


def _dot3(a, b):
    ah, al = _split(a)
    bh, bl = _split(b)
    return (jnp.dot(ah, bh, preferred_element_type=F32)
            + jnp.dot(al, bh, preferred_element_type=F32)
            + jnp.dot(ah, bl, preferred_element_type=F32))


def _sigmoid(x):
    return 0.5 * jnp.tanh(0.5 * x) + 0.5


def _silu(x):
    return x * _sigmoid(x)


def _softplus(x):
    return jnp.maximum(x, 0.0) + jnp.log1p(jnp.exp(-jnp.abs(x)))


def _level_sizes():
    s, out = CHUNK // 2, []
    while s >= 1:
        out.append(s)
        s //= 2
    return out


def _np_masks():
    i = np.arange(CHUNK)[:, None]
    j = np.arange(CHUNK)[None, :]
    ms = [(j <= i), (j < i)]
    for s in _level_sizes():
        bi, bj = i // s, j // s
        ms.append((bi % 2 == 1) & (bj == bi - 1))
    return np.stack(ms).astype(np.float32)


def _np_gla_sel():
    i = np.arange(CHUNK)[:, None]
    m = np.arange(CHUNK)[None, :]
    blocks = [(m <= i), (m > i)]
    for s in _level_sizes():
        p = i // s
        odd = (p % 2 == 1)
        b_odd = p * s - 1
        b_even = (p + 1) * s - 1
        blocks.append(np.where(odd, (m > b_odd) & (m <= i), (m > i) & (m <= b_even)))
    return np.concatenate(blocks, axis=0).astype(np.float32)


def _inproj_kernel(x_ref, g_ref, wm_ref, ws_ref, om_ref, os_ref, h_ref):
    @pl.when(pl.program_id(1) == 0)
    def _():
        x = x_ref[...]
        h = x * lax.rsqrt(jnp.mean(x * x, axis=-1, keepdims=True) + RMS_EPS) * g_ref[...]
        hb = h.astype(BF16)
        h_ref[...] = hb
        os_ref[...] = jnp.dot(hb, ws_ref[...], preferred_element_type=F32)

    om_ref[...] = jnp.dot(h_ref[...], wm_ref[...], preferred_element_type=F32).astype(BF16)


def _inproj(x, g, w_main, w_small, tm=1024, tn=MAIN_W // 4):
    t, d = x.shape
    grid = (t // tm, MAIN_W // tn)
    return pl.pallas_call(
        _inproj_kernel,
        grid=grid,
        in_specs=[
            pl.BlockSpec((tm, d), lambda i, j: (i, 0)),
            pl.BlockSpec((1, d), lambda i, j: (0, 0)),
            pl.BlockSpec((d, tn), lambda i, j: (0, j)),
            pl.BlockSpec((d, LANES), lambda i, j: (0, 0)),
        ],
        out_specs=[
            pl.BlockSpec((tm, tn), lambda i, j: (i, j)),
            pl.BlockSpec((tm, LANES), lambda i, j: (i, 0)),
        ],
        out_shape=[jax.ShapeDtypeStruct((t, MAIN_W), BF16),
                   jax.ShapeDtypeStruct((t, LANES), F32)],
        scratch_shapes=[pltpu.VMEM((tm, d), BF16)],
        compiler_params=_cparams(("parallel", "arbitrary")),
        name="inproj",
    )(x, g, w_main, w_small)


def _gla_kernel(q_ref, k_ref, v_ref, r_ref, sm_ref, wgu_ref, bg_ref, ng_ref, sel_ref, msk_ref,
                o_ref, st_ref, e_ref, *, n_chunks, group):
    @pl.when(pl.program_id(1) == 0)
    def _():
        st_ref[...] = jnp.zeros_like(st_ref)

    n_lv = len(_level_sizes())
    sel = sel_ref[...]
    eye2 = msk_ref[0] - msk_ref[1]
    z = _dot3(sm_ref[...], wgu_ref[...]) + bg_ref[...]
    la_all = (jnp.minimum(z, 0.0) - jnp.log1p(jnp.exp(-jnp.abs(z)))) * (1.0 / GLA_GATE_NORM)
    ng = ng_ref[...]
    zero_b = jnp.zeros((CHUNK, GLA_DK), BF16)

    def head_blocks(x2):
        return jnp.concatenate([jnp.concatenate([x2[:, :GLA_DK], zero_b], axis=1),
                                jnp.concatenate([zero_b, x2[:, GLA_DK:]], axis=1)], axis=0)

    sv, upd, qe, e_last = {}, {}, {}, {}
    for c0 in range(0, n_chunks, group):
        chunks = range(c0, min(c0 + group, n_chunks))
        for c in chunks:
            hi, lo = _split(la_all[c * CHUNK:(c + 1) * CHUNK])
            e_ref[c] = jnp.exp(jnp.dot(sel, jnp.concatenate([hi, lo], axis=0), preferred_element_type=F32))
        units = [(c, hp) for c in chunks for hp in range(GLA_HEADS // 2)]
        lhs, rhs, vb, kd = {}, {}, {}, {}
        for (c, hp) in units:
            rows = slice(c * CHUNK, (c + 1) * CHUNK)
            cols2 = slice(hp * 2 * GLA_DK, (hp + 1) * 2 * GLA_DK)
            q2 = q_ref[rows, cols2].astype(F32) * (GLA_DK ** -0.5)
            k2 = k_ref[rows, cols2].astype(F32)
            ls, rs = [q2.astype(BF16)], [head_blocks(k2.astype(BF16))]
            for li in range(n_lv):
                e_l = e_ref[c, (2 + li) * CHUNK:(3 + li) * CHUNK, cols2]
                ls.append((q2 * e_l).astype(BF16))
                rs.append(head_blocks((k2 * e_l).astype(BF16)))
            lhs[(c, hp)], rhs[(c, hp)] = ls, rs
            e_q = e_ref[c, 0:CHUNK, cols2]
            qe[(c, hp)] = (q2 * e_q).astype(BF16)
            e_last[(c, hp)] = e_q[CHUNK - 1:CHUNK, :]
            kd[(c, hp)] = (k2 * e_ref[c, CHUNK:2 * CHUNK, cols2]).astype(BF16)
            vb[(c, hp)] = v_ref[rows, cols2]
        raw = {u: [_dot_nt(a, b_) for a, b_ in zip(lhs[u], rhs[u])] for u in units}
        for u in units:
            upd[u] = [_dot_tn(vb[u][:, h * GLA_DK:(h + 1) * GLA_DK], kd[u][:, h * GLA_DK:(h + 1) * GLA_DK])
                      for h in range(2)]
        for u in units:
            sc = eye2 * raw[u][0]
            for li in range(n_lv):
                sc = sc + msk_ref[2 + li] * raw[u][1 + li]
            sv[u] = jnp.dot(sc.astype(BF16), head_blocks(vb[u]), preferred_element_type=F32)

    for c in range(n_chunks):
        rows = slice(c * CHUNK, (c + 1) * CHUNK)
        for hp in range(GLA_HEADS // 2):
            u = (c, hp)
            for hh in range(2):
                h = 2 * hp + hh
                cols = slice(h * GLA_DK, (h + 1) * GLA_DK)
                lc = slice(hh * GLA_DK, (hh + 1) * GLA_DK)
                st = st_ref[h]
                o = _dot_nt(qe[u][:, lc], st) + sv[u][:, lc]
                st_ref[h] = st * e_last[u][:, lc] + upd[u][hh]
                on = o * lax.rsqrt(jnp.mean(o * o, axis=-1, keepdims=True) + RMS_EPS) * ng
                o_ref[rows, cols] = (on * _silu(r_ref[rows, cols].astype(F32))).astype(BF16)


def _gla(main, small, wgu_pad, b_gate, norm_g, b, s, ts=512, group=2):
    t = main.shape[0]
    nst = s // ts
    row = lambda bi, si: bi * nst + si
    sel = jnp.asarray(np.tile(_np_gla_sel(), (1, 2)), BF16)
    msk = jnp.asarray(np.tile(_np_masks(), (1, 1, 2)), F32)

    def col(cb):
        return pl.BlockSpec((ts, HEAD_W), lambda bi, si: (row(bi, si), cb))

    const2 = lambda bi, si: (0, 0)
    return pl.pallas_call(
        functools.partial(_gla_kernel, n_chunks=ts // CHUNK, group=group),
        grid=(b, nst),
        in_specs=[
            col(CB_GLA_Q), col(CB_GLA_K), col(CB_GLA_V), col(CB_GLA_R),
            pl.BlockSpec((ts, LANES), lambda bi, si: (row(bi, si), 0)),
            pl.BlockSpec(wgu_pad.shape, const2),
            pl.BlockSpec(b_gate.shape, const2),
            pl.BlockSpec(norm_g.shape, const2),
            pl.BlockSpec(sel.shape, const2),
            pl.BlockSpec(msk.shape, lambda bi, si: (0, 0, 0)),
        ],
        out_specs=pl.BlockSpec((ts, HEAD_W), lambda bi, si: (row(bi, si), 0)),
        out_shape=jax.ShapeDtypeStruct((t, HEAD_W), BF16),
        scratch_shapes=[pltpu.VMEM((GLA_HEADS, GLA_DK, GLA_DK), F32),
                        pltpu.VMEM((ts // CHUNK, sel.shape[0], HEAD_W), F32)],
        compiler_params=_cparams(("parallel", "arbitrary")),
        name="gla",
    )(main, main, main, main, small, wgu_pad, b_gate, norm_g, sel, msk)


def _att_kernel(q_ref, kp_ref, kc_ref, vp_ref, vc_ref, bias_ref, o_ref, kk_ref, vv_ref,
                *, n_chunks, left, unroll):
    band = left + CHUNK
    kk_ref[0:left, :] = kp_ref[...]
    kk_ref[left:2 * left, :] = kc_ref[...]
    vv_ref[0:left, :] = vp_ref[...]
    vv_ref[left:2 * left, :] = vc_ref[...]
    first = pl.program_id(1) == 0
    lane = lax.broadcasted_iota(jnp.int32, (CHUNK, LANES), 1)
    low_half = lane < ATT_HD

    def chunk_body(ci, carry, *, masked):
        r0s = [pl.multiple_of((ci * unroll + cc) * CHUNK, CHUNK) for cc in range(unroll)]
        scores = []
        for r0 in r0s:
            for hp in range(ATT_HEADS // 2):
                cols = slice(hp * LANES, (hp + 1) * LANES)
                q2 = q_ref[pl.ds(r0, CHUNK), cols].astype(F32) * (ATT_HD ** -0.5)
                k2 = kk_ref[pl.ds(r0, band), cols]
                for par in range(2):
                    qm = jnp.where(low_half if par == 0 else ~low_half, q2, 0.0).astype(BF16)
                    scores.append(_dot_nt(qm, k2))
        probs = []
        for cc, r0 in enumerate(r0s):
            if masked:
                jband = lax.broadcasted_iota(jnp.int32, (CHUNK, band), 1)
                valid = jband >= left - r0
            for h in range(ATT_HEADS):
                sc = scores[cc * ATT_HEADS + h] + bias_ref[h]
                if masked:
                    sc = jnp.where(valid, sc, -jnp.inf)
                m = jnp.max(sc, axis=-1, keepdims=True)
                p = jnp.exp(sc - m)
                probs.append((p.astype(BF16), 1.0 / jnp.sum(p, axis=-1, keepdims=True)))
        for cc, r0 in enumerate(r0s):
            for hp in range(ATT_HEADS // 2):
                cols = slice(hp * LANES, (hp + 1) * LANES)
                v2 = vv_ref[pl.ds(r0, band), cols]
                pr = [probs[cc * ATT_HEADS + 2 * hp + par] for par in range(2)]
                outs = [jnp.dot(p, v2, preferred_element_type=F32) * rl for p, rl in pr]
                o_ref[pl.ds(r0, CHUNK), cols] = jnp.where(low_half, outs[0], outs[1]).astype(BF16)
        return carry

    @pl.when(first)
    def _():
        lax.fori_loop(0, n_chunks // unroll, functools.partial(chunk_body, masked=True), 0)

    @pl.when(jnp.logical_not(first))
    def _():
        lax.fori_loop(0, n_chunks // unroll, functools.partial(chunk_body, masked=False), 0)


def _att(main, bias, b, s, unroll=2):
    t = main.shape[0]
    left = ATT_LEFT_CHUNKS * CHUNK
    tq = left
    nst = s // tq
    cur = lambda cb: pl.BlockSpec((tq, HEAD_W), lambda bi, si: (bi * nst + si, cb))
    prev = lambda cb: pl.BlockSpec((tq, HEAD_W),
                                   lambda bi, si: (bi * nst + jnp.maximum(si - 1, 0), cb))
    return pl.pallas_call(
        functools.partial(_att_kernel, n_chunks=tq // CHUNK, left=left, unroll=unroll),
        grid=(b, nst),
        in_specs=[
            cur(CB_ATT_Q), prev(CB_ATT_K), cur(CB_ATT_K), prev(CB_ATT_V), cur(CB_ATT_V),
            pl.BlockSpec(bias.shape, lambda bi, si: (0, 0, 0)),
        ],
        out_specs=pl.BlockSpec((tq, HEAD_W), lambda bi, si: (bi * nst + si, 0)),
        out_shape=jax.ShapeDtypeStruct((t, HEAD_W), BF16),
        scratch_shapes=[pltpu.VMEM((2 * left, HEAD_W), BF16),
                        pltpu.VMEM((2 * left, HEAD_W), BF16)],
        compiler_params=_cparams(("parallel", "arbitrary")),
        name="band_att",
    )(main, main, main, main, main, bias)


def _gdn_kernel(q_ref, k_ref, v_ref, z_ref, sm_ref, cw_ref, alog_ref, dtb_ref, ng_ref, msk_ref,
                o_ref, st_ref, xb_ref, qkv_ref, *, n_chunks, group):
    ts = n_chunks * CHUNK
    pad = 8

    @pl.when(pl.program_id(1) == 0)
    def _():
        st_ref[...] = jnp.zeros_like(st_ref)
        xb_ref[0:pad, :] = jnp.zeros((pad, 3 * HEAD_W), F32)

    xb_ref[pad:pad + ts, 0:HEAD_W] = q_ref[...].astype(F32)
    xb_ref[pad:pad + ts, HEAD_W:2 * HEAD_W] = k_ref[...].astype(F32)
    xb_ref[pad:pad + ts, 2 * HEAD_W:3 * HEAD_W] = v_ref[...].astype(F32)
    acc = None
    for j in range(CONV_W):
        off = pad - (CONV_W - 1) + j
        term = xb_ref[off:off + ts, :] * cw_ref[j:j + 1, :]
        acc = term if acc is None else acc + term
    xb_ref[0:pad, :] = xb_ref[ts:ts + pad, :]
    qkv_ref[...] = _silu(acc)

    sm = sm_ref[...]
    la_all = -jnp.exp(alog_ref[...]) * _softplus(sm + dtb_ref[...])
    beta_all = _sigmoid(sm)
    incl = msk_ref[0]
    strict = msk_ref[1]
    tril_b = incl.astype(BF16)
    eye = incl - strict
    ng = ng_ref[...]
    n_lv = len(_level_sizes())

    ops = {}
    for c0 in range(0, n_chunks, group):
        chunks = range(c0, min(c0 + group, n_chunks))
        pairs = [(c, h) for c in chunks for h in range(GDN_HEADS)]
        gc_alls = {c: _dot_sel(tril_b, la_all[c * CHUNK:(c + 1) * CHUNK]) for c in chunks}
        pre = {}
        for (c, h) in pairs:
            rows = slice(c * CHUNK, (c + 1) * CHUNK)
            la_col = la_all[rows, SMALL_A + h:SMALL_A + h + 1]
            beta = beta_all[rows, SMALL_B + h:SMALL_B + h + 1]
            gc = gc_alls[c][:, SMALL_A + h:SMALL_A + h + 1]
            gc_last = gc[CHUNK - 1:CHUNK, :]
            q = qkv_ref[rows, h * GDN_DK:(h + 1) * GDN_DK]
            k = qkv_ref[rows, HEAD_W + h * GDN_DK:HEAD_W + (h + 1) * GDN_DK]
            v = qkv_ref[rows, 2 * HEAD_W + h * GDN_DK:2 * HEAD_W + (h + 1) * GDN_DK]
            qn = q * lax.rsqrt(jnp.sum(q * q, axis=-1, keepdims=True) + RMS_EPS) * (GDN_DK ** -0.5)
            kn = k * lax.rsqrt(jnp.sum(k * k, axis=-1, keepdims=True) + RMS_EPS)
            kb = kn * beta
            pre[(c, h)] = dict(
                lcs=la_col * strict, qn=qn.astype(BF16), kn=kn.astype(BF16), kb=kb.astype(BF16),
                rhs=jnp.concatenate([v * beta, kb * jnp.exp(gc)], axis=1).astype(BF16),
                qg=qn * jnp.exp(gc), kd=(kn * jnp.exp(gc_last - gc)).astype(BF16),
                g_last=jnp.exp(gc_last))
        mm = {p: (_dot_sel(tril_b, pre[p]["lcs"]), _dot_nt(pre[p]["kb"], pre[p]["kn"]),
                  _dot_nt(pre[p]["qn"], pre[p]["kn"])) for p in pairs}
        a_mat, attn, p_inv = {}, {}, {}
        for p in pairs:
            dmat, kk, qk = mm[p]
            gamma = jnp.exp(dmat)
            a_mat[p] = strict * (kk * gamma)
            attn[p] = (incl * (qk * gamma)).astype(BF16)
            p_inv[p] = eye - a_mat[p] * msk_ref[2 + n_lv - 1]
        for li in range(n_lv - 2, -1, -1):
            half = {p: _dot(p_inv[p], a_mat[p] * msk_ref[2 + li]) for p in pairs}
            p_inv = {p: p_inv[p] - _dot(half[p], p_inv[p]) for p in pairs}
        sol = {p: _dot(p_inv[p], pre[p]["rhs"]).astype(BF16) for p in pairs}
        for p in pairs:
            nw = _dot_tn(pre[p]["kd"], sol[p])
            ao = _dot(attn[p], sol[p])
            ops[p] = (nw[:, :GDN_DK], nw[:, GDN_DK:].astype(BF16), ao[:, :GDN_DK],
                      (pre[p]["qg"] - ao[:, GDN_DK:]).astype(BF16), pre[p]["g_last"])

    for c in range(n_chunks):
        rows = slice(c * CHUNK, (c + 1) * CHUNK)
        res = {}
        for h in range(GDN_HEADS):
            n_mat, w2, o2, q2, g_last = ops[(c, h)]
            res[h] = jnp.dot(jnp.concatenate([w2, q2], axis=0), st_ref[h].astype(BF16),
                             preferred_element_type=F32)
        for h in range(GDN_HEADS):
            cols = slice(h * GDN_DK, (h + 1) * GDN_DK)
            n_mat, w2, o2, q2, g_last = ops[(c, h)]
            st_ref[h] = st_ref[h] * g_last - res[h][:GDN_DK] + n_mat
            o = res[h][GDN_DK:] + o2
            on = o * lax.rsqrt(jnp.mean(o * o, axis=-1, keepdims=True) + RMS_EPS) * ng
            o_ref[rows, cols] = (on * _silu(z_ref[rows, cols].astype(F32))).astype(BF16)


def _gdn(main, small, conv_w, alog_vec, dtb_vec, norm_g, b, s, ts=512, group=4):
    t = main.shape[0]
    nst = s // ts
    row = lambda bi, si: bi * nst + si
    msk = jnp.asarray(_np_masks(), F32)

    def col(cb):
        return pl.BlockSpec((ts, HEAD_W), lambda bi, si: (row(bi, si), cb))

    const2 = lambda bi, si: (0, 0)
    return pl.pallas_call(
        functools.partial(_gdn_kernel, n_chunks=ts // CHUNK, group=group),
        grid=(b, nst),
        in_specs=[
            col(CB_GDN_Q), col(CB_GDN_K), col(CB_GDN_V), col(CB_GDN_Z),
            pl.BlockSpec((ts, LANES), lambda bi, si: (row(bi, si), 0)),
            pl.BlockSpec(conv_w.shape, const2),
            pl.BlockSpec(alog_vec.shape, const2),
            pl.BlockSpec(dtb_vec.shape, const2),
            pl.BlockSpec(norm_g.shape, const2),
            pl.BlockSpec(msk.shape, lambda bi, si: (0, 0, 0)),
        ],
        out_specs=pl.BlockSpec((ts, HEAD_W), lambda bi, si: (row(bi, si), 0)),
        out_shape=jax.ShapeDtypeStruct((t, HEAD_W), BF16),
        scratch_shapes=[pltpu.VMEM((GDN_HEADS, GDN_DK, GDN_DK), F32),
                        pltpu.VMEM((ts + 8, 3 * HEAD_W), F32),
                        pltpu.VMEM((ts, 3 * HEAD_W), F32)],
        compiler_params=_cparams(("parallel", "arbitrary")),
        name="gdn",
    )(main, main, main, main, small, conv_w, alog_vec, dtb_vec, norm_g, msk)


def _merge_kernel(x_ref, m0_ref, m1_ref, m2_ref, og_ref, oa_ref, od_ref, wg_ref, wa_ref, wd_ref,
                  wo_ref, g2_ref, x1_ref, h2_ref):
    y = (_sigmoid(m0_ref[...].astype(F32)) * jnp.dot(og_ref[...], wg_ref[...], preferred_element_type=F32)
         + _sigmoid(m1_ref[...].astype(F32)) * jnp.dot(oa_ref[...], wa_ref[...], preferred_element_type=F32)
         + _sigmoid(m2_ref[...].astype(F32)) * jnp.dot(od_ref[...], wd_ref[...], preferred_element_type=F32))
    x1 = x_ref[...] + jnp.dot(y.astype(BF16), wo_ref[...], preferred_element_type=F32)
    x1_ref[...] = x1
    h2 = x1 * lax.rsqrt(jnp.mean(x1 * x1, axis=-1, keepdims=True) + RMS_EPS) * g2_ref[...]
    h2_ref[...] = h2.astype(BF16)


def _merge(x, main, o_gla, o_att, o_gdn, wg, wa, wd, wo, g2, tm=512):
    t, d = x.shape
    rowb = lambda i: (i, 0)
    const = lambda i: (0, 0)
    return pl.pallas_call(
        _merge_kernel,
        grid=(t // tm,),
        in_specs=[
            pl.BlockSpec((tm, d), rowb),
            pl.BlockSpec((tm, d), lambda i: (i, 0)),
            pl.BlockSpec((tm, d), lambda i: (i, 1)),
            pl.BlockSpec((tm, d), lambda i: (i, 2)),
            pl.BlockSpec((tm, HEAD_W), rowb),
            pl.BlockSpec((tm, HEAD_W), rowb),
            pl.BlockSpec((tm, HEAD_W), rowb),
            pl.BlockSpec(wg.shape, const),
            pl.BlockSpec(wa.shape, const),
            pl.BlockSpec(wd.shape, const),
            pl.BlockSpec(wo.shape, const),
            pl.BlockSpec(g2.shape, const),
        ],
        out_specs=[pl.BlockSpec((tm, d), rowb), pl.BlockSpec((tm, d), rowb)],
        out_shape=[jax.ShapeDtypeStruct((t, d), F32), jax.ShapeDtypeStruct((t, d), BF16)],
        compiler_params=_cparams(("parallel",)),
        name="merge_out",
    )(x, main, main, main, o_gla, o_att, o_gdn, wg, wa, wd, wo, g2)


def _ffn_kernel(x1_ref, h2_ref, wgate_ref, wup_ref, wdown_ref, gf_ref, o_ref, acc_ref, *, final_norm):
    j = pl.program_id(1)

    @pl.when(j == 0)
    def _():
        acc_ref[...] = x1_ref[...]

    h2 = h2_ref[...]
    gate = jnp.dot(h2, wgate_ref[...], preferred_element_type=F32)
    up = jnp.dot(h2, wup_ref[...], preferred_element_type=F32)
    acc_ref[...] += jnp.dot((_silu(gate) * up).astype(BF16), wdown_ref[...],
                            preferred_element_type=F32)

    @pl.when(j == pl.num_programs(1) - 1)
    def _():
        x2 = acc_ref[...]
        if final_norm:
            x2 = x2 * lax.rsqrt(jnp.mean(x2 * x2, axis=-1, keepdims=True) + RMS_EPS) * gf_ref[...]
        o_ref[...] = x2


def _ffn(x1, h2, w_in, w_out, gf, final_norm, tm=512, n_h=2):
    t, d = x1.shape
    hidden = w_out.shape[0]
    th = hidden // n_h
    return pl.pallas_call(
        functools.partial(_ffn_kernel, final_norm=final_norm),
        grid=(t // tm, n_h),
        in_specs=[
            pl.BlockSpec((tm, d), lambda i, j: (i, 0)),
            pl.BlockSpec((tm, d), lambda i, j: (i, 0)),
            pl.BlockSpec((d, th), lambda i, j: (0, j)),
            pl.BlockSpec((d, th), lambda i, j: (0, j + n_h)),
            pl.BlockSpec((th, d), lambda i, j: (j, 0)),
            pl.BlockSpec(gf.shape, lambda i, j: (0, 0)),
        ],
        out_specs=pl.BlockSpec((tm, d), lambda i, j: (i, 0)),
        out_shape=jax.ShapeDtypeStruct((t, d), F32),
        scratch_shapes=[pltpu.VMEM((tm, d), F32)],
        compiler_params=_cparams(("parallel", "arbitrary")),
        name="ffn",
    )(x1, h2, w_in, w_in, w_out, gf)


def _split_w_in(w_in):
    widths = (512, 512, 512, 512, GLA_GATE_RANK, 512, 512, 512, 512, 512, 512, 512,
              GDN_HEADS, GDN_HEADS, w_in.shape[1] - (11 * 512 + GLA_GATE_RANK + 2 * GDN_HEADS))
    bounds = np.cumsum(widths)[:-1]
    (gq, gk, gv, gr, glr, aq, ak, av, dq, dk, dv, dz, da, db, mg) = jnp.split(w_in, bounds, axis=1)
    main = jnp.concatenate([mg, gq, gk, gv, gr, aq, ak, av, dq, dk, dv, dz], axis=1)
    small = jnp.concatenate([glr, da, db], axis=1)
    small = jnp.pad(small, ((0, 0), (0, LANES - small.shape[1])))
    return main.astype(BF16), small.astype(BF16)


def _band_bias(rel_bias):
    left = ATT_LEFT_CHUNKS * CHUNK
    band = left + CHUNK
    n_ext = left + 2 * CHUNK - 1
    ext = jnp.concatenate([rel_bias, jnp.repeat(rel_bias[:, -1:], n_ext - rel_bias.shape[1], axis=1)],
                          axis=1)[:, ::-1]
    rows = [ext[:, CHUNK - 1 - i:CHUNK - 1 - i + band] for i in range(CHUNK)]
    return jnp.stack(rows, axis=1).astype(F32)


def _lane_vec(vals, offset):
    out = jnp.zeros((1, LANES), F32)
    return lax.dynamic_update_slice(out, vals.reshape(1, -1).astype(F32), (0, offset))


def kernel(x, mix_norm_g, w_in, gla_w_gate_up, gla_b_gate, gla_norm_g, att_rel_bias, gdn_conv_w,
           gdn_a_log, gdn_dt_bias, gdn_norm_g, w_branch_gla, w_branch_att, w_branch_gdn, w_out,
           ffn_norm_g, w_ffn_in, w_ffn_out, final_norm_g):
    b, s, d = x.shape
    depth = w_in.shape[0]
    xt = x.reshape(b * s, d)
    for l in range(depth):
        w_main, w_small = _split_w_in(w_in[l])
        main, small = _inproj(xt, mix_norm_g[l].reshape(1, d), w_main, w_small)
        wgu_pad = jnp.pad(gla_w_gate_up[l], ((0, LANES - GLA_GATE_RANK), (0, 0)))
        o_gla = _gla(main, small, wgu_pad, gla_b_gate[l].reshape(1, -1),
                     gla_norm_g[l].reshape(1, -1), b, s)
        o_att = _att(main, _band_bias(att_rel_bias[l]), b, s)
        o_gdn = _gdn(main, small, gdn_conv_w[l], _lane_vec(gdn_a_log[l], SMALL_A),
                     _lane_vec(gdn_dt_bias[l], SMALL_A), gdn_norm_g[l].reshape(1, -1), b, s)
        x1, h2 = _merge(xt, main, o_gla, o_att, o_gdn, w_branch_gla[l].astype(BF16),
                        w_branch_att[l].astype(BF16), w_branch_gdn[l].astype(BF16),
                        w_out[l].astype(BF16), ffn_norm_g[l].reshape(1, d))
        xt = _ffn(x1, h2, w_ffn_in[l].astype(BF16), w_ffn_out[l].astype(BF16),
                  final_norm_g.reshape(1, d), final_norm=(l == depth - 1))
    return xt.reshape(b, s, d)
```

```python
import functools

import numpy as np
import jax
import jax.numpy as jnp
from jax import lax
from jax.experimental import pallas as pl
from jax.experimental.pallas import tpu as pltpu

F32 = jnp.float32
BF16 = jnp.bfloat16

CHUNK = 64
RMS_EPS = 1e-6
GLA_HEADS = 4
GLA_DK = 128
GLA_GATE_RANK = 16
GLA_GATE_NORM = 16.0
ATT_HEADS = 8
ATT_HD = 64
ATT_LEFT_CHUNKS = 8
REL_MAX = 256
GDN_HEADS = 4
GDN_DK = 128
CONV_W = 4
HEAD_W = 512
LANES = 128
SMALL_A = GLA_GATE_RANK
SMALL_B = GLA_GATE_RANK + GDN_HEADS

CB_GLA_Q, CB_GLA_K, CB_GLA_V, CB_GLA_R = 6, 7, 8, 9
CB_ATT_Q, CB_ATT_K, CB_ATT_V = 10, 11, 12
CB_GDN_Q, CB_GDN_K, CB_GDN_V, CB_GDN_Z = 13, 14, 15, 16
MAIN_W = 17 * HEAD_W

VMEM_LIMIT = 56 * 1024 * 1024


def _cparams(sem):
    return pltpu.CompilerParams(dimension_semantics=sem, vmem_limit_bytes=VMEM_LIMIT)


def _dot(a, b):
    return jnp.dot(a.astype(BF16), b.astype(BF16), preferred_element_type=F32)


def _dot_nt(a, b):
    return lax.dot_general(a.astype(BF16), b.astype(BF16), (((1,), (1,)), ((), ())),
                           preferred_element_type=F32)


def _dot_tn(a, b):
    return lax.dot_general(a.astype(BF16), b.astype(BF16), (((0,), (0,)), ((), ())),
                           preferred_element_type=F32)


def _split(x):
    hi = x.astype(BF16)
    lo = (x - hi.astype(F32)).astype(BF16)
    return hi, lo


def _dot_sel(m_bf16, x):
    hi, lo = _split(x)
    return (jnp.dot(m_bf16, hi, preferred_element_type=F32)
            + jnp.dot(m_bf16, lo, preferred_element_type=F32))


def _dot3(a, b):
    ah, al = _split(a)
    bh, bl = _split(b)
    return (jnp.dot(ah, bh, preferred_element_type=F32)
            + jnp.dot(al, bh, preferred_element_type=F32)
            + jnp.dot(ah, bl, preferred_element_type=F32))


def _sigmoid(x):
    return 0.5 * jnp.tanh(0.5 * x) + 0.5


def _silu(x):
    return x * _sigmoid(x)


def _softplus(x):
    return jnp.maximum(x, 0.0) + jnp.log1p(jnp.exp(-jnp.abs(x)))


def _level_sizes():
    s, out = CHUNK // 2, []
    while s >= 1:
        out.append(s)
        s //= 2
    return out


def _np_masks():
    i = np.arange(CHUNK)[:, None]
    j = np.arange(CHUNK)[None, :]
    ms = [(j <= i), (j < i)]
    for s in _level_sizes():
        bi, bj = i // s, j // s
        ms.append((bi % 2 == 1) & (bj == bi - 1))
    return np.stack(ms).astype(np.float32)


def _np_gla_sel():
    i = np.arange(CHUNK)[:, None]
    m = np.arange(CHUNK)[None, :]
    blocks = [(m <= i), (m > i)]
    for s in _level_sizes():
        p = i // s
        odd = (p % 2 == 1)
        b_odd = p * s - 1
        b_even = (p + 1) * s - 1
        blocks.append(np.where(odd, (m > b_odd) & (m <= i), (m > i) & (m <= b_even)))
    return np.concatenate(blocks, axis=0).astype(np.float32)


def _inproj_kernel(x_ref, g_ref, wm_ref, ws_ref, om_ref, os_ref, h_ref):
    @pl.when(pl.program_id(1) == 0)
    def _():
        x = x_ref[...]
        h = x * lax.rsqrt(jnp.mean(x * x, axis=-1, keepdims=True) + RMS_EPS) * g_ref[...]
        hb = h.astype(BF16)
        h_ref[...] = hb
        os_ref[...] = jnp.dot(hb, ws_ref[...], preferred_element_type=F32)

    om_ref[...] = jnp.dot(h_ref[...], wm_ref[...], preferred_element_type=F32).astype(BF16)


def _inproj(x, g, w_main, w_small, tm=1024, tn=MAIN_W // 2):
    t, d = x.shape
    grid = (t // tm, MAIN_W // tn)
    return pl.pallas_call(
        _inproj_kernel,
        grid=grid,
        in_specs=[
            pl.BlockSpec((tm, d), lambda i, j: (i, 0)),
            pl.BlockSpec((1, d), lambda i, j: (0, 0)),
            pl.BlockSpec((d, tn), lambda i, j: (0, j)),
            pl.BlockSpec((d, LANES), lambda i, j: (0, 0)),
        ],
        out_specs=[
            pl.BlockSpec((tm, tn), lambda i, j: (i, j)),
            pl.BlockSpec((tm, LANES), lambda i, j: (i, 0)),
        ],
        out_shape=[jax.ShapeDtypeStruct((t, MAIN_W), BF16),
                   jax.ShapeDtypeStruct((t, LANES), F32)],
        scratch_shapes=[pltpu.VMEM((tm, d), BF16)],
        compiler_params=_cparams(("parallel", "arbitrary")),
        name="inproj",
    )(x, g, w_main, w_small)


def _gla_kernel(q_ref, k_ref, v_ref, r_ref, sm_ref, wgu_ref, bg_ref, ng_ref, sel_ref, msk_ref,
                o_ref, st_ref, e_ref, *, n_chunks, group):
    @pl.when(pl.program_id(1) == 0)
    def _():
        st_ref[...] = jnp.zeros_like(st_ref)

    n_lv = len(_level_sizes())
    sel = sel_ref[...]
    eye2 = msk_ref[0] - msk_ref[1]
    z = _dot3(sm_ref[...], wgu_ref[...]) + bg_ref[...]
    la_all = (jnp.minimum(z, 0.0) - jnp.log1p(jnp.exp(-jnp.abs(z)))) * (1.0 / GLA_GATE_NORM)
    ng = ng_ref[...]
    zero_b = jnp.zeros((CHUNK, GLA_DK), BF16)

    def head_blocks(x2):
        return jnp.concatenate([jnp.concatenate([x2[:, :GLA_DK], zero_b], axis=1),
                                jnp.concatenate([zero_b, x2[:, GLA_DK:]], axis=1)], axis=0)

    sv, upd, qe, e_last = {}, {}, {}, {}
    for c0 in range(0, n_chunks, group):
        chunks = range(c0, min(c0 + group, n_chunks))
        for c in chunks:
            hi, lo = _split(la_all[c * CHUNK:(c + 1) * CHUNK])
            e_ref[c] = jnp.exp(jnp.dot(sel, jnp.concatenate([hi, lo], axis=0), preferred_element_type=F32))
        units = [(c, hp) for c in chunks for hp in range(GLA_HEADS // 2)]
        lhs, rhs, vb, kd = {}, {}, {}, {}
        for (c, hp) in units:
            rows = slice(c * CHUNK, (c + 1) * CHUNK)
            cols2 = slice(hp * 2 * GLA_DK, (hp + 1) * 2 * GLA_DK)
            q2 = q_ref[rows, cols2].astype(F32) * (GLA_DK ** -0.5)
            k2 = k_ref[rows, cols2].astype(F32)
            ls, rs = [q2.astype(BF16)], [head_blocks(k2.astype(BF16))]
            for li in range(n_lv):
                e_l = e_ref[c, (2 + li) * CHUNK:(3 + li) * CHUNK, cols2]
                ls.append((q2 * e_l).astype(BF16))
                rs.append(head_blocks((k2 * e_l).astype(BF16)))
            lhs[(c, hp)], rhs[(c, hp)] = ls, rs
            e_q = e_ref[c, 0:CHUNK, cols2]
            qe[(c, hp)] = (q2 * e_q).astype(BF16)
            e_last[(c, hp)] = e_q[CHUNK - 1:CHUNK, :]
            kd[(c, hp)] = (k2 * e_ref[c, CHUNK:2 * CHUNK, cols2]).astype(BF16)
            vb[(c, hp)] = v_ref[rows, cols2]
        raw = {u: [_dot_nt(a, b_) for a, b_ in zip(lhs[u], rhs[u])] for u in units}
        for u in units:
            upd[u] = [_dot_tn(vb[u][:, h * GLA_DK:(h + 1) * GLA_DK], kd[u][:, h * GLA_DK:(h + 1) * GLA_DK])
                      for h in range(2)]
        for u in units:
            sc = eye2 * raw[u][0]
            for li in range(n_lv):
                sc = sc + msk_ref[2 + li] * raw[u][1 + li]
            sv[u] = jnp.dot(sc.astype(BF16), head_blocks(vb[u]), preferred_element_type=F32)

    for c in range(n_chunks):
        rows = slice(c * CHUNK, (c + 1) * CHUNK)
        for hp in range(GLA_HEADS // 2):
            u = (c, hp)
            for hh in range(2):
                h = 2 * hp + hh
                cols = slice(h * GLA_DK, (h + 1) * GLA_DK)
                lc = slice(hh * GLA_DK, (hh + 1) * GLA_DK)
                st = st_ref[h]
                o = _dot_nt(qe[u][:, lc], st) + sv[u][:, lc]
                st_ref[h] = st * e_last[u][:, lc] + upd[u][hh]
                on = o * lax.rsqrt(jnp.mean(o * o, axis=-1, keepdims=True) + RMS_EPS) * ng
                o_ref[rows, cols] = (on * _silu(r_ref[rows, cols].astype(F32))).astype(BF16)


def _gla(main, small, wgu_pad, b_gate, norm_g, b, s, ts=512, group=2):
    t = main.shape[0]
    nst = s // ts
    row = lambda bi, si: bi * nst + si
    sel = jnp.asarray(np.tile(_np_gla_sel(), (1, 2)), BF16)
    msk = jnp.asarray(np.tile(_np_masks(), (1, 1, 2)), F32)

    def col(cb):
        return pl.BlockSpec((ts, HEAD_W), lambda bi, si: (row(bi, si), cb))

    const2 = lambda bi, si: (0, 0)
    return pl.pallas_call(
        functools.partial(_gla_kernel, n_chunks=ts // CHUNK, group=group),
        grid=(b, nst),
        in_specs=[
            col(CB_GLA_Q), col(CB_GLA_K), col(CB_GLA_V), col(CB_GLA_R),
            pl.BlockSpec((ts, LANES), lambda bi, si: (row(bi, si), 0)),
            pl.BlockSpec(wgu_pad.shape, const2),
            pl.BlockSpec(b_gate.shape, const2),
            pl.BlockSpec(norm_g.shape, const2),
            pl.BlockSpec(sel.shape, const2),
            pl.BlockSpec(msk.shape, lambda bi, si: (0, 0, 0)),
        ],
        out_specs=pl.BlockSpec((ts, HEAD_W), lambda bi, si: (row(bi, si), 0)),
        out_shape=jax.ShapeDtypeStruct((t, HEAD_W), BF16),
        scratch_shapes=[pltpu.VMEM((GLA_HEADS, GLA_DK, GLA_DK), F32),
                        pltpu.VMEM((ts // CHUNK, sel.shape[0], HEAD_W), F32)],
        compiler_params=_cparams(("parallel", "arbitrary")),
        name="gla",
    )(main, main, main, main, small, wgu_pad, b_gate, norm_g, sel, msk)


def _att_kernel(q_ref, kp_ref, kc_ref, vp_ref, vc_ref, bias_ref, o_ref, kk_ref, vv_ref,
                *, n_chunks, left, unroll):
    band = left + CHUNK
    kk_ref[0:left, :] = kp_ref[...]
    kk_ref[left:2 * left, :] = kc_ref[...]
    vv_ref[0:left, :] = vp_ref[...]
    vv_ref[left:2 * left, :] = vc_ref[...]
    first = pl.program_id(1) == 0
    lane = lax.broadcasted_iota(jnp.int32, (CHUNK, LANES), 1)
    low_half = lane < ATT_HD

    def chunk_body(ci, carry, *, masked):
        r0s = [pl.multiple_of((ci * unroll + cc) * CHUNK, CHUNK) for cc in range(unroll)]
        scores = []
        for r0 in r0s:
            for hp in range(ATT_HEADS // 2):
                cols = slice(hp * LANES, (hp + 1) * LANES)
                q2 = q_ref[pl.ds(r0, CHUNK), cols].astype(F32) * (ATT_HD ** -0.5)
                k2 = kk_ref[pl.ds(r0, band), cols]
                for par in range(2):
                    qm = jnp.where(low_half if par == 0 else ~low_half, q2, 0.0).astype(BF16)
                    scores.append(_dot_nt(qm, k2))
        probs = []
        for cc, r0 in enumerate(r0s):
            if masked:
                jband = lax.broadcasted_iota(jnp.int32, (CHUNK, band), 1)
                valid = jband >= left - r0
            for h in range(ATT_HEADS):
                sc = scores[cc * ATT_HEADS + h] + bias_ref[h]
                if masked:
                    sc = jnp.where(valid, sc, -jnp.inf)
                m = jnp.max(sc, axis=-1, keepdims=True)
                p = jnp.exp(sc - m)
                probs.append((p.astype(BF16), 1.0 / jnp.sum(p, axis=-1, keepdims=True)))
        for cc, r0 in enumerate(r0s):
            for hp in range(ATT_HEADS // 2):
                cols = slice(hp * LANES, (hp + 1) * LANES)
                v2 = vv_ref[pl.ds(r0, band), cols]
                pr = [probs[cc * ATT_HEADS + 2 * hp + par] for par in range(2)]
                outs = [jnp.dot(p, v2, preferred_element_type=F32) * rl for p, rl in pr]
                o_ref[pl.ds(r0, CHUNK), cols] = jnp.where(low_half, outs[0], outs[1]).astype(BF16)
        return carry

    @pl.when(first)
    def _():
        lax.fori_loop(0, n_chunks // unroll, functools.partial(chunk_body, masked=True), 0)

    @pl.when(jnp.logical_not(first))
    def _():
        lax.fori_loop(0, n_chunks // unroll, functools.partial(chunk_body, masked=False), 0)


def _att(main, bias, b, s, unroll=2):
    t = main.shape[0]
    left = ATT_LEFT_CHUNKS * CHUNK
    tq = left
    nst = s // tq
    cur = lambda cb: pl.BlockSpec((tq, HEAD_W), lambda bi, si: (bi * nst + si, cb))
    prev = lambda cb: pl.BlockSpec((tq, HEAD_W),
                                   lambda bi, si: (bi * nst + jnp.maximum(si - 1, 0), cb))
    return pl.pallas_call(
        functools.partial(_att_kernel, n_chunks=tq // CHUNK, left=left, unroll=unroll),
        grid=(b, nst),
        in_specs=[
            cur(CB_ATT_Q), prev(CB_ATT_K), cur(CB_ATT_K), prev(CB_ATT_V), cur(CB_ATT_V),
            pl.BlockSpec(bias.shape, lambda bi, si: (0, 0, 0)),
        ],
        out_specs=pl.BlockSpec((tq, HEAD_W), lambda bi, si: (bi * nst + si, 0)),
        out_shape=jax.ShapeDtypeStruct((t, HEAD_W), BF16),
        scratch_shapes=[pltpu.VMEM((2 * left, HEAD_W), BF16),
                        pltpu.VMEM((2 * left, HEAD_W), BF16)],
        compiler_params=_cparams(("parallel", "arbitrary")),
        name="band_att",
    )(main, main, main, main, main, bias)


def _gdn_kernel(q_ref, k_ref, v_ref, z_ref, sm_ref, cw_ref, alog_ref, dtb_ref, ng_ref, msk_ref,
                o_ref, st_ref, xb_ref, qkv_ref, *, n_chunks, group):
    ts = n_chunks * CHUNK
    pad = 8

    @pl.when(pl.program_id(1) == 0)
    def _():
        st_ref[...] = jnp.zeros_like(st_ref)
        xb_ref[0:pad, :] = jnp.zeros((pad, 3 * HEAD_W), F32)

    xb_ref[pad:pad + ts, 0:HEAD_W] = q_ref[...].astype(F32)
    xb_ref[pad:pad + ts, HEAD_W:2 * HEAD_W] = k_ref[...].astype(F32)
    xb_ref[pad:pad + ts, 2 * HEAD_W:3 * HEAD_W] = v_ref[...].astype(F32)
    acc = None
    for j in range(CONV_W):
        off = pad - (CONV_W - 1) + j
        term = xb_ref[off:off + ts, :] * cw_ref[j:j + 1, :]
        acc = term if acc is None else acc + term
    xb_ref[0:pad, :] = xb_ref[ts:ts + pad, :]
    qkv_ref[...] = _silu(acc)

    sm = sm_ref[...]
    la_all = -jnp.exp(alog_ref[...]) * _softplus(sm + dtb_ref[...])
    beta_all = _sigmoid(sm)
    incl = msk_ref[0]
    strict = msk_ref[1]
    tril_b = incl.astype(BF16)
    eye = incl - strict
    ng = ng_ref[...]
    n_lv = len(_level_sizes())

    ops = {}
    for c0 in range(0, n_chunks, group):
        chunks = range(c0, min(c0 + group, n_chunks))
        pairs = [(c, h) for c in chunks for h in range(GDN_HEADS)]
        gc_alls = {c: _dot_sel(tril_b, la_all[c * CHUNK:(c + 1) * CHUNK]) for c in chunks}
        pre = {}
        for (c, h) in pairs:
            rows = slice(c * CHUNK, (c + 1) * CHUNK)
            la_col = la_all[rows, SMALL_A + h:SMALL_A + h + 1]
            beta = beta_all[rows, SMALL_B + h:SMALL_B + h + 1]
            gc = gc_alls[c][:, SMALL_A + h:SMALL_A + h + 1]
            gc_last = gc[CHUNK - 1:CHUNK, :]
            q = qkv_ref[rows, h * GDN_DK:(h + 1) * GDN_DK]
            k = qkv_ref[rows, HEAD_W + h * GDN_DK:HEAD_W + (h + 1) * GDN_DK]
            v = qkv_ref[rows, 2 * HEAD_W + h * GDN_DK:2 * HEAD_W + (h + 1) * GDN_DK]
            qn = q * lax.rsqrt(jnp.sum(q * q, axis=-1, keepdims=True) + RMS_EPS) * (GDN_DK ** -0.5)
            kn = k * lax.rsqrt(jnp.sum(k * k, axis=-1, keepdims=True) + RMS_EPS)
            kb = kn * beta
            pre[(c, h)] = dict(
                lcs=la_col * strict, qn=qn.astype(BF16), kn=kn.astype(BF16), kb=kb.astype(BF16),
                rhs=jnp.concatenate([v * beta, kb * jnp.exp(gc)], axis=1).astype(BF16),
                qg=qn * jnp.exp(gc), kd=(kn * jnp.exp(gc_last - gc)).astype(BF16),
                g_last=jnp.exp(gc_last))
        mm = {p: (_dot_sel(tril_b, pre[p]["lcs"]), _dot_nt(pre[p]["kb"], pre[p]["kn"]),
                  _dot_nt(pre[p]["qn"], pre[p]["kn"])) for p in pairs}
        a_mat, attn, p_inv = {}, {}, {}
        for p in pairs:
            dmat, kk, qk = mm[p]
            gamma = jnp.exp(dmat)
            a_mat[p] = strict * (kk * gamma)
            attn[p] = (incl * (qk * gamma)).astype(BF16)
            p_inv[p] = eye - a_mat[p] * msk_ref[2 + n_lv - 1]
        for li in range(n_lv - 2, -1, -1):
            half = {p: _dot(p_inv[p], a_mat[p] * msk_ref[2 + li]) for p in pairs}
            p_inv = {p: p_inv[p] - _dot(half[p], p_inv[p]) for p in pairs}
        sol = {p: _dot(p_inv[p], pre[p]["rhs"]).astype(BF16) for p in pairs}
        for p in pairs:
            nw = _dot_tn(pre[p]["kd"], sol[p])
            ao = _dot(attn[p], sol[p])
            ops[p] = (nw[:, :GDN_DK], nw[:, GDN_DK:].astype(BF16), ao[:, :GDN_DK],
                      (pre[p]["qg"] - ao[:, GDN_DK:]).astype(BF16), pre[p]["g_last"])

    for c in range(n_chunks):
        rows = slice(c * CHUNK, (c + 1) * CHUNK)
        res = {}
        for h in range(GDN_HEADS):
            n_mat, w2, o2, q2, g_last = ops[(c, h)]
            res[h] = jnp.dot(jnp.concatenate([w2, q2], axis=0), st_ref[h].astype(BF16),
                             preferred_element_type=F32)
        for h in range(GDN_HEADS):
            cols = slice(h * GDN_DK, (h + 1) * GDN_DK)
            n_mat, w2, o2, q2, g_last = ops[(c, h)]
            st_ref[h] = st_ref[h] * g_last - res[h][:GDN_DK] + n_mat
            o = res[h][GDN_DK:] + o2
            on = o * lax.rsqrt(jnp.mean(o * o, axis=-1, keepdims=True) + RMS_EPS) * ng
            o_ref[rows, cols] = (on * _silu(z_ref[rows, cols].astype(F32))).astype(BF16)


def _gdn(main, small, conv_w, alog_vec, dtb_vec, norm_g, b, s, ts=512, group=4):
    t = main.shape[0]
    nst = s // ts
    row = lambda bi, si: bi * nst + si
    msk = jnp.asarray(_np_masks(), F32)

    def col(cb):
        return pl.BlockSpec((ts, HEAD_W), lambda bi, si: (row(bi, si), cb))

    const2 = lambda bi, si: (0, 0)
    return pl.pallas_call(
        functools.partial(_gdn_kernel, n_chunks=ts // CHUNK, group=group),
        grid=(b, nst),
        in_specs=[
            col(CB_GDN_Q), col(CB_GDN_K), col(CB_GDN_V), col(CB_GDN_Z),
            pl.BlockSpec((ts, LANES), lambda bi, si: (row(bi, si), 0)),
            pl.BlockSpec(conv_w.shape, const2),
            pl.BlockSpec(alog_vec.shape, const2),
            pl.BlockSpec(dtb_vec.shape, const2),
            pl.BlockSpec(norm_g.shape, const2),
            pl.BlockSpec(msk.shape, lambda bi, si: (0, 0, 0)),
        ],
        out_specs=pl.BlockSpec((ts, HEAD_W), lambda bi, si: (row(bi, si), 0)),
        out_shape=jax.ShapeDtypeStruct((t, HEAD_W), BF16),
        scratch_shapes=[pltpu.VMEM((GDN_HEADS, GDN_DK, GDN_DK), F32),
                        pltpu.VMEM((ts + 8, 3 * HEAD_W), F32),
                        pltpu.VMEM((ts, 3 * HEAD_W), F32)],
        compiler_params=_cparams(("parallel", "arbitrary")),
        name="gdn",
    )(main, main, main, main, small, conv_w, alog_vec, dtb_vec, norm_g, msk)


def _merge_kernel(x_ref, m0_ref, m1_ref, m2_ref, og_ref, oa_ref, od_ref, wg_ref, wa_ref, wd_ref,
                  wo_ref, g2_ref, x1_ref, h2_ref):
    y = (_sigmoid(m0_ref[...].astype(F32)) * jnp.dot(og_ref[...], wg_ref[...], preferred_element_type=F32)
         + _sigmoid(m1_ref[...].astype(F32)) * jnp.dot(oa_ref[...], wa_ref[...], preferred_element_type=F32)
         + _sigmoid(m2_ref[...].astype(F32)) * jnp.dot(od_ref[...], wd_ref[...], preferred_element_type=F32))
    x1 = x_ref[...] + jnp.dot(y.astype(BF16), wo_ref[...], preferred_element_type=F32)
    x1_ref[...] = x1
    h2 = x1 * lax.rsqrt(jnp.mean(x1 * x1, axis=-1, keepdims=True) + RMS_EPS) * g2_ref[...]
    h2_ref[...] = h2.astype(BF16)


def _merge(x, main, o_gla, o_att, o_gdn, wg, wa, wd, wo, g2, tm=512):
    t, d = x.shape
    rowb = lambda i: (i, 0)
    const = lambda i: (0, 0)
    return pl.pallas_call(
        _merge_kernel,
        grid=(t // tm,),
        in_specs=[
            pl.BlockSpec((tm, d), rowb),
            pl.BlockSpec((tm, d), lambda i: (i, 0)),
            pl.BlockSpec((tm, d), lambda i: (i, 1)),
            pl.BlockSpec((tm, d), lambda i: (i, 2)),
            pl.BlockSpec((tm, HEAD_W), rowb),
            pl.BlockSpec((tm, HEAD_W), rowb),
            pl.BlockSpec((tm, HEAD_W), rowb),
            pl.BlockSpec(wg.shape, const),
            pl.BlockSpec(wa.shape, const),
            pl.BlockSpec(wd.shape, const),
            pl.BlockSpec(wo.shape, const),
            pl.BlockSpec(g2.shape, const),
        ],
        out_specs=[pl.BlockSpec((tm, d), rowb), pl.BlockSpec((tm, d), rowb)],
        out_shape=[jax.ShapeDtypeStruct((t, d), F32), jax.ShapeDtypeStruct((t, d), BF16)],
        compiler_params=_cparams(("parallel",)),
        name="merge_out",
    )(x, main, main, main, o_gla, o_att, o_gdn, wg, wa, wd, wo, g2)


def _ffn_kernel(x1_ref, h2_ref, wgu_ref, wdown_ref, gf_ref, o_ref, acc_ref, *, final_norm):
    j = pl.program_id(1)

    @pl.when(j == 0)
    def _():
        acc_ref[...] = x1_ref[...]

    th = wdown_ref.shape[0]
    gu = jnp.dot(h2_ref[...], wgu_ref[...], preferred_element_type=F32)
    acc_ref[...] += jnp.dot((_silu(gu[:, :th]) * gu[:, th:]).astype(BF16), wdown_ref[...],
                            preferred_element_type=F32)

    @pl.when(j == pl.num_programs(1) - 1)
    def _():
        x2 = acc_ref[...]
        if final_norm:
            x2 = x2 * lax.rsqrt(jnp.mean(x2 * x2, axis=-1, keepdims=True) + RMS_EPS) * gf_ref[...]
        o_ref[...] = x2


def _ffn(x1, h2, w_in, w_out, gf, final_norm, tm=512, n_h=2):
    t, d = x1.shape
    hidden = w_out.shape[0]
    th = hidden // n_h
    w_gu = jnp.concatenate([w_in[:, c:c + th] for j in range(n_h) for c in (j * th, hidden + j * th)], axis=1)
    return pl.pallas_call(
        functools.partial(_ffn_kernel, final_norm=final_norm),
        grid=(t // tm, n_h),
        in_specs=[
            pl.BlockSpec((tm, d), lambda i, j: (i, 0)),
            pl.BlockSpec((tm, d), lambda i, j: (i, 0)),
            pl.BlockSpec((d, 2 * th), lambda i, j: (0, j)),
            pl.BlockSpec((th, d), lambda i, j: (j, 0)),
            pl.BlockSpec(gf.shape, lambda i, j: (0, 0)),
        ],
        out_specs=pl.BlockSpec((tm, d), lambda i, j: (i, 0)),
        out_shape=jax.ShapeDtypeStruct((t, d), F32),
        scratch_shapes=[pltpu.VMEM((tm, d), F32)],
        compiler_params=_cparams(("parallel", "arbitrary")),
        name="ffn",
    )(x1, h2, w_gu, w_out, gf)


def _split_w_in(w_in):
    widths = (512, 512, 512, 512, GLA_GATE_RANK, 512, 512, 512, 512, 512, 512, 512,
              GDN_HEADS, GDN_HEADS, w_in.shape[1] - (11 * 512 + GLA_GATE_RANK + 2 * GDN_HEADS))
    bounds = np.cumsum(widths)[:-1]
    (gq, gk, gv, gr, glr, aq, ak, av, dq, dk, dv, dz, da, db, mg) = jnp.split(w_in, bounds, axis=1)
    main = jnp.concatenate([mg, gq, gk, gv, gr, aq, ak, av, dq, dk, dv, dz], axis=1)
    small = jnp.concatenate([glr, da, db], axis=1)
    small = jnp.pad(small, ((0, 0), (0, LANES - small.shape[1])))
    return main.astype(BF16), small.astype(BF16)


def _band_bias(rel_bias):
    left = ATT_LEFT_CHUNKS * CHUNK
    band = left + CHUNK
    n_ext = left + 2 * CHUNK - 1
    ext = jnp.concatenate([rel_bias, jnp.repeat(rel_bias[:, -1:], n_ext - rel_bias.shape[1], axis=1)],
                          axis=1)[:, ::-1]
    rows = [ext[:, CHUNK - 1 - i:CHUNK - 1 - i + band] for i in range(CHUNK)]
    return jnp.stack(rows, axis=1).astype(F32)


def _lane_vec(vals, offset):
    out = jnp.zeros((1, LANES), F32)
    return lax.dynamic_update_slice(out, vals.reshape(1, -1).astype(F32), (0, offset))


def kernel(x, mix_norm_g, w_in, gla_w_gate_up, gla_b_gate, gla_norm_g, att_rel_bias, gdn_conv_w,
           gdn_a_log, gdn_dt_bias, gdn_norm_g, w_branch_gla, w_branch_att, w_branch_gdn, w_out,
           ffn_norm_g, w_ffn_in, w_ffn_out, final_norm_g):
    b, s, d = x.shape
    depth = w_in.shape[0]
    xt = x.reshape(b * s, d)
    for l in range(depth):
        w_main, w_small = _split_w_in(w_in[l])
        main, small = _inproj(xt, mix_norm_g[l].reshape(1, d), w_main, w_small)
        wgu_pad = jnp.pad(gla_w_gate_up[l], ((0, LANES - GLA_GATE_RANK), (0, 0)))
        o_gla = _gla(main, small, wgu_pad, gla_b_gate[l].reshape(1, -1),
                     gla_norm_g[l].reshape(1, -1), b, s)
        o_att = _att(main, _band_bias(att_rel_bias[l]), b, s)
        o_gdn = _gdn(main, small, gdn_conv_w[l], _lane_vec(gdn_a_log[l], SMALL_A),
                     _lane_vec(gdn_dt_bias[l], SMALL_A), gdn_norm_g[l].reshape(1, -1), b, s)
        x1, h2 = _merge(xt, main, o_gla, o_att, o_gdn, w_branch_gla[l].astype(BF16),
                        w_branch_att[l].astype(BF16), w_branch_gdn[l].astype(BF16),
                        w_out[l].astype(BF16), ffn_norm_g[l].reshape(1, d))
        xt = _ffn(x1, h2, w_ffn_in[l].astype(BF16), w_ffn_out[l].astype(BF16),
                  final_norm_g.reshape(1, d), final_norm=(l == depth - 1))
    return xt.reshape(b, s, d)
```

```python
import functools

import numpy as np
import jax
import jax.numpy as jnp
from jax import lax
from jax.experimental import pallas as pl
from jax.experimental.pallas import tpu as pltpu

F32 = jnp.float32
BF16 = jnp.bfloat16

CHUNK = 64
RMS_EPS = 1e-6
GLA_HEADS = 4
GLA_DK = 128
GLA_GATE_RANK = 16
GLA_GATE_NORM = 16.0
ATT_HEADS = 8
ATT_HD = 64
ATT_LEFT_CHUNKS = 8
REL_MAX = 256
GDN_HEADS = 4
GDN_DK = 128
CONV_W = 4
HEAD_W = 512
LANES = 128
SMALL_A = GLA_GATE_RANK
SMALL_B = GLA_GATE_RANK + GDN_HEADS

CB_GLA_Q, CB_GLA_K, CB_GLA_V, CB_GLA_R = 6, 7, 8, 9
CB_ATT_Q, CB_ATT_K, CB_ATT_V = 10, 11, 12
CB_GDN_Q, CB_GDN_K, CB_GDN_V, CB_GDN_Z = 13, 14, 15, 16
MAIN_W = 17 * HEAD_W

VMEM_LIMIT = 56 * 1024 * 1024


def _cparams(sem):
    return pltpu.CompilerParams(dimension_semantics=sem, vmem_limit_bytes=VMEM_LIMIT)


def _dot(a, b):
    return jnp.dot(a.astype(BF16), b.astype(BF16), preferred_element_type=F32)


def _dot_nt(a, b):
    return lax.dot_general(a.astype(BF16), b.astype(BF16), (((1,), (1,)), ((), ())),
                           preferred_element_type=F32)


def _dot_tn(a, b):
    return lax.dot_general(a.astype(BF16), b.astype(BF16), (((0,), (0,)), ((), ())),
                           preferred_element_type=F32)


def _split(x):
    hi = x.astype(BF16)
    lo = (x - hi.astype(F32)).astype(BF16)
    return hi, lo


def _dot_sel(m_bf16, x):
    hi, lo = _split(x)
    return (jnp.dot(m_bf16, hi, preferred_element_type=F32)
            + jnp.dot(m_bf16, lo, preferred_element_type=F32))


def _dot3(a, b):
    ah, al = _split(a)
    bh, bl = _split(b)
    return (jnp.dot(ah, bh, preferred_element_type=F32)
            + jnp.dot(al, bh, preferred_element_type=F32)
            + jnp.dot(ah, bl, preferred_element_type=F32))


def _sigmoid(x):
    return 0.5 * jnp.tanh(0.5 * x) + 0.5


def _silu(x):
    return x * _sigmoid(x)


def _softplus(x):
    return jnp.maximum(x, 0.0) + jnp.log1p(jnp.exp(-jnp.abs(x)))


def _level_sizes():
    s, out = CHUNK // 2, []
    while s >= 1:
        out.append(s)
        s //= 2
    return out


def _np_masks():
    i = np.arange(CHUNK)[:, None]
    j = np.arange(CHUNK)[None, :]
    ms = [(j <= i), (j < i)]
    for s in _level_sizes():
        bi, bj = i // s, j // s
        ms.append((bi % 2 == 1) & (bj == bi - 1))
    return np.stack(ms).astype(np.float32)


def _np_gla_sel():
    i = np.arange(CHUNK)[:, None]
    m = np.arange(CHUNK)[None, :]
    blocks = [(m <= i), (m > i)]
    for s in _level_sizes():
        p = i // s
        odd = (p % 2 == 1)
        b_odd = p * s - 1
        b_even = (p + 1) * s - 1
        blocks.append(np.where(odd, (m > b_odd) & (m <= i), (m > i) & (m <= b_even)))
    return np.concatenate(blocks, axis=0).astype(np.float32)


def _inproj_kernel(x_ref, g_ref, wm_ref, ws_ref, om_ref, os_ref, h_ref):
    @pl.when(pl.program_id(1) == 0)
    def _():
        x = x_ref[...]
        h = x * lax.rsqrt(jnp.mean(x * x, axis=-1, keepdims=True) + RMS_EPS) * g_ref[...]
        hb = h.astype(BF16)
        h_ref[...] = hb
        os_ref[...] = jnp.dot(hb, ws_ref[...], preferred_element_type=F32)

    om_ref[...] = jnp.dot(h_ref[...], wm_ref[...], preferred_element_type=F32).astype(BF16)


def _inproj(x, g, w_main, w_small, tm=1024, tn=MAIN_W // 2):
    t, d = x.shape
    grid = (t // tm, MAIN_W // tn)
    return pl.pallas_call(
        _inproj_kernel,
        grid=grid,
        in_specs=[
            pl.BlockSpec((tm, d), lambda i, j: (i, 0)),
            pl.BlockSpec((1, d), lambda i, j: (0, 0)),
            pl.BlockSpec((d, tn), lambda i, j: (0, j)),
            pl.BlockSpec((d, LANES), lambda i, j: (0, 0)),
        ],
        out_specs=[
            pl.BlockSpec((tm, tn), lambda i, j: (i, j)),
            pl.BlockSpec((tm, LANES), lambda i, j: (i, 0)),
        ],
        out_shape=[jax.ShapeDtypeStruct((t, MAIN_W), BF16),
                   jax.ShapeDtypeStruct((t, LANES), F32)],
        scratch_shapes=[pltpu.VMEM((tm, d), BF16)],
        compiler_params=_cparams(("parallel", "arbitrary")),
        name="inproj",
    )(x, g, w_main, w_small)


def _gla_kernel(q_ref, k_ref, v_ref, r_ref, sm_ref, wgu_ref, bg_ref, ng_ref, sel_ref, msk_ref,
                o_ref, st_ref, e_ref, *, n_chunks, group):
    @pl.when(pl.program_id(1) == 0)
    def _():
        st_ref[...] = jnp.zeros_like(st_ref)

    n_lv = len(_level_sizes())
    sel = sel_ref[...]
    eye2 = msk_ref[0] - msk_ref[1]
    z = _dot3(sm_ref[...], wgu_ref[...]) + bg_ref[...]
    la_all = (jnp.minimum(z, 0.0) - jnp.log1p(jnp.exp(-jnp.abs(z)))) * (1.0 / GLA_GATE_NORM)
    ng = ng_ref[...]
    zero_b = jnp.zeros((CHUNK, GLA_DK), BF16)

    def head_blocks(x2):
        return jnp.concatenate([jnp.concatenate([x2[:, :GLA_DK], zero_b], axis=1),
                                jnp.concatenate([zero_b, x2[:, GLA_DK:]], axis=1)], axis=0)

    sv, upd, qe, e_last = {}, {}, {}, {}
    for c0 in range(0, n_chunks, group):
        chunks = range(c0, min(c0 + group, n_chunks))
        for c in chunks:
            hi, lo = _split(la_all[c * CHUNK:(c + 1) * CHUNK])
            e_ref[c] = jnp.exp(jnp.dot(sel, jnp.concatenate([hi, lo], axis=0), preferred_element_type=F32))
        units = [(c, hp) for c in chunks for hp in range(GLA_HEADS // 2)]
        lhs, rhs, vb, kd = {}, {}, {}, {}
        for (c, hp) in units:
            rows = slice(c * CHUNK, (c + 1) * CHUNK)
            cols2 = slice(hp * 2 * GLA_DK, (hp + 1) * 2 * GLA_DK)
            q2 = q_ref[rows, cols2].astype(F32) * (GLA_DK ** -0.5)
            k2 = k_ref[rows, cols2].astype(F32)
            ls, rs = [q2.astype(BF16)], [head_blocks(k2.astype(BF16))]
            for li in range(n_lv):
                e_l = e_ref[c, (2 + li) * CHUNK:(3 + li) * CHUNK, cols2]
                ls.append((q2 * e_l).astype(BF16))
                rs.append(head_blocks((k2 * e_l).astype(BF16)))
            lhs[(c, hp)], rhs[(c, hp)] = ls, rs
            e_q = e_ref[c, 0:CHUNK, cols2]
            qe[(c, hp)] = (q2 * e_q).astype(BF16)
            e_last[(c, hp)] = e_q[CHUNK - 1:CHUNK, :]
            kd[(c, hp)] = (k2 * e_ref[c, CHUNK:2 * CHUNK, cols2]).astype(BF16)
            vb[(c, hp)] = v_ref[rows, cols2]
        raw = {u: [_dot_nt(a, b_) for a, b_ in zip(lhs[u], rhs[u])] for u in units}
        for u in units:
            upd[u] = [_dot_tn(vb[u][:, h * GLA_DK:(h + 1) * GLA_DK], kd[u][:, h * GLA_DK:(h + 1) * GLA_DK])
                      for h in range(2)]
        for u in units:
            sc = eye2 * raw[u][0]
            for li in range(n_lv):
                sc = sc + msk_ref[2 + li] * raw[u][1 + li]
            sv[u] = jnp.dot(sc.astype(BF16), head_blocks(vb[u]), preferred_element_type=F32)

    for c in range(n_chunks):
        rows = slice(c * CHUNK, (c + 1) * CHUNK)
        for hp in range(GLA_HEADS // 2):
            u = (c, hp)
            for hh in range(2):
                h = 2 * hp + hh
                cols = slice(h * GLA_DK, (h + 1) * GLA_DK)
                lc = slice(hh * GLA_DK, (hh + 1) * GLA_DK)
                st = st_ref[h]
                o = _dot_nt(qe[u][:, lc], st) + sv[u][:, lc]
                st_ref[h] = st * e_last[u][:, lc] + upd[u][hh]
                on = o * lax.rsqrt(jnp.mean(o * o, axis=-1, keepdims=True) + RMS_EPS) * ng
                o_ref[rows, cols] = (on * _silu(r_ref[rows, cols].astype(F32))).astype(BF16)


def _gla(main, small, wgu_pad, b_gate, norm_g, b, s, ts=512, group=2):
    t = main.shape[0]
    nst = s // ts
    row = lambda bi, si: bi * nst + si
    sel = jnp.asarray(np.tile(_np_gla_sel(), (1, 2)), BF16)
    msk = jnp.asarray(np.tile(_np_masks(), (1, 1, 2)), F32)

    def col(cb):
        return pl.BlockSpec((ts, HEAD_W), lambda bi, si: (row(bi, si), cb))

    const2 = lambda bi, si: (0, 0)
    return pl.pallas_call(
        functools.partial(_gla_kernel, n_chunks=ts // CHUNK, group=group),
        grid=(b, nst),
        in_specs=[
            col(CB_GLA_Q), col(CB_GLA_K), col(CB_GLA_V), col(CB_GLA_R),
            pl.BlockSpec((ts, LANES), lambda bi, si: (row(bi, si), 0)),
            pl.BlockSpec(wgu_pad.shape, const2),
            pl.BlockSpec(b_gate.shape, const2),
            pl.BlockSpec(norm_g.shape, const2),
            pl.BlockSpec(sel.shape, const2),
            pl.BlockSpec(msk.shape, lambda bi, si: (0, 0, 0)),
        ],
        out_specs=pl.BlockSpec((ts, HEAD_W), lambda bi, si: (row(bi, si), 0)),
        out_shape=jax.ShapeDtypeStruct((t, HEAD_W), BF16),
        scratch_shapes=[pltpu.VMEM((GLA_HEADS, GLA_DK, GLA_DK), F32),
                        pltpu.VMEM((ts // CHUNK, sel.shape[0], HEAD_W), F32)],
        compiler_params=_cparams(("parallel", "arbitrary")),
        name="gla",
    )(main, main, main, main, small, wgu_pad, b_gate, norm_g, sel, msk)


def _att_kernel(q_ref, kp_ref, kc_ref, vp_ref, vc_ref, bias_ref, o_ref, kk_ref, vv_ref,
                *, n_chunks, left, unroll):
    band = left + CHUNK
    kk_ref[0:left, :] = kp_ref[...]
    kk_ref[left:2 * left, :] = kc_ref[...]
    vv_ref[0:left, :] = vp_ref[...]
    vv_ref[left:2 * left, :] = vc_ref[...]
    first = pl.program_id(1) == 0
    lane = lax.broadcasted_iota(jnp.int32, (CHUNK, LANES), 1)
    low_half = lane < ATT_HD

    def chunk_body(ci, carry, *, masked):
        r0s = [pl.multiple_of((ci * unroll + cc) * CHUNK, CHUNK) for cc in range(unroll)]
        scores = []
        for r0 in r0s:
            for hp in range(ATT_HEADS // 2):
                cols = slice(hp * LANES, (hp + 1) * LANES)
                q2 = q_ref[pl.ds(r0, CHUNK), cols].astype(F32) * (ATT_HD ** -0.5)
                k2 = kk_ref[pl.ds(r0, band), cols]
                for par in range(2):
                    qm = jnp.where(low_half if par == 0 else ~low_half, q2, 0.0).astype(BF16)
                    scores.append(_dot_nt(qm, k2))
        probs = []
        for cc, r0 in enumerate(r0s):
            if masked:
                jband = lax.broadcasted_iota(jnp.int32, (CHUNK, band), 1)
                valid = jband >= left - r0
            for h in range(ATT_HEADS):
                sc = scores[cc * ATT_HEADS + h] + bias_ref[h]
                if masked:
                    sc = jnp.where(valid, sc, -jnp.inf)
                m = jnp.max(sc, axis=-1, keepdims=True)
                p = jnp.exp(sc - m)
                probs.append((p.astype(BF16), 1.0 / jnp.sum(p, axis=-1, keepdims=True)))
        for cc, r0 in enumerate(r0s):
            for hp in range(ATT_HEADS // 2):
                cols = slice(hp * LANES, (hp + 1) * LANES)
                v2 = vv_ref[pl.ds(r0, band), cols]
                pr = [probs[cc * ATT_HEADS + 2 * hp + par] for par in range(2)]
                outs = [jnp.dot(p, v2, preferred_element_type=F32) * rl for p, rl in pr]
                o_ref[pl.ds(r0, CHUNK), cols] = jnp.where(low_half, outs[0], outs[1]).astype(BF16)
        return carry

    @pl.when(first)
    def _():
        lax.fori_loop(0, n_chunks // unroll, functools.partial(chunk_body, masked=True), 0)

    @pl.when(jnp.logical_not(first))
    def _():
        lax.fori_loop(0, n_chunks // unroll, functools.partial(chunk_body, masked=False), 0)


def _att(main, bias, b, s, unroll=2):
    t = main.shape[0]
    left = ATT_LEFT_CHUNKS * CHUNK
    tq = left
    nst = s // tq
    cur = lambda cb: pl.BlockSpec((tq, HEAD_W), lambda bi, si: (bi * nst + si, cb))
    prev = lambda cb: pl.BlockSpec((tq, HEAD_W),
                                   lambda bi, si: (bi * nst + jnp.maximum(si - 1, 0), cb))
    return pl.pallas_call(
        functools.partial(_att_kernel, n_chunks=tq // CHUNK, left=left, unroll=unroll),
        grid=(b, nst),
        in_specs=[
            cur(CB_ATT_Q), prev(CB_ATT_K), cur(CB_ATT_K), prev(CB_ATT_V), cur(CB_ATT_V),
            pl.BlockSpec(bias.shape, lambda bi, si: (0, 0, 0)),
        ],
        out_specs=pl.BlockSpec((tq, HEAD_W), lambda bi, si: (bi * nst + si, 0)),
        out_shape=jax.ShapeDtypeStruct((t, HEAD_W), BF16),
        scratch_shapes=[pltpu.VMEM((2 * left, HEAD_W), BF16),
                        pltpu.VMEM((2 * left, HEAD_W), BF16)],
        compiler_params=_cparams(("parallel", "arbitrary")),
        name="band_att",
    )(main, main, main, main, main, bias)


def _gdn_kernel(q_ref, k_ref, v_ref, z_ref, sm_ref, cw_ref, alog_ref, dtb_ref, ng_ref, msk_ref,
                o_ref, st_ref, xb_ref, qkv_ref, *, n_chunks, group):
    ts = n_chunks * CHUNK
    pad = 8

    @pl.when(pl.program_id(1) == 0)
    def _():
        st_ref[...] = jnp.zeros_like(st_ref)
        xb_ref[0:pad, :] = jnp.zeros((pad, 3 * HEAD_W), F32)

    xb_ref[pad:pad + ts, 0:HEAD_W] = q_ref[...].astype(F32)
    xb_ref[pad:pad + ts, HEAD_W:2 * HEAD_W] = k_ref[...].astype(F32)
    xb_ref[pad:pad + ts, 2 * HEAD_W:3 * HEAD_W] = v_ref[...].astype(F32)
    acc = None
    for j in range(CONV_W):
        off = pad - (CONV_W - 1) + j
        term = xb_ref[off:off + ts, :] * cw_ref[j:j + 1, :]
        acc = term if acc is None else acc + term
    xb_ref[0:pad, :] = xb_ref[ts:ts + pad, :]
    qkv_ref[...] = _silu(acc)

    sm = sm_ref[...]
    la_all = -jnp.exp(alog_ref[...]) * _softplus(sm + dtb_ref[...])
    beta_all = _sigmoid(sm)
    incl = msk_ref[0]
    strict = msk_ref[1]
    tril_b = incl.astype(BF16)
    eye = incl - strict
    ng = ng_ref[...]
    n_lv = len(_level_sizes())

    ops = {}
    for c0 in range(0, n_chunks, group):
        chunks = range(c0, min(c0 + group, n_chunks))
        pairs = [(c, h) for c in chunks for h in range(GDN_HEADS)]
        gc_alls = {c: _dot_sel(tril_b, la_all[c * CHUNK:(c + 1) * CHUNK]) for c in chunks}
        pre = {}
        for (c, h) in pairs:
            rows = slice(c * CHUNK, (c + 1) * CHUNK)
            la_col = la_all[rows, SMALL_A + h:SMALL_A + h + 1]
            beta = beta_all[rows, SMALL_B + h:SMALL_B + h + 1]
            gc = gc_alls[c][:, SMALL_A + h:SMALL_A + h + 1]
            gc_last = gc[CHUNK - 1:CHUNK, :]
            q = qkv_ref[rows, h * GDN_DK:(h + 1) * GDN_DK]
            k = qkv_ref[rows, HEAD_W + h * GDN_DK:HEAD_W + (h + 1) * GDN_DK]
            v = qkv_ref[rows, 2 * HEAD_W + h * GDN_DK:2 * HEAD_W + (h + 1) * GDN_DK]
            qn = q * lax.rsqrt(jnp.sum(q * q, axis=-1, keepdims=True) + RMS_EPS) * (GDN_DK ** -0.5)
            kn = k * lax.rsqrt(jnp.sum(k * k, axis=-1, keepdims=True) + RMS_EPS)
            kb = kn * beta
            pre[(c, h)] = dict(
                lcs=la_col * strict, qn=qn.astype(BF16), kn=kn.astype(BF16), kb=kb.astype(BF16),
                rhs=jnp.concatenate([v * beta, kb * jnp.exp(gc)], axis=1).astype(BF16),
                qg=qn * jnp.exp(gc), kd=(kn * jnp.exp(gc_last - gc)).astype(BF16),
                g_last=jnp.exp(gc_last))
        mm = {p: (_dot_sel(tril_b, pre[p]["lcs"]), _dot_nt(pre[p]["kb"], pre[p]["kn"]),
                  _dot_nt(pre[p]["qn"], pre[p]["kn"])) for p in pairs}
        a_mat, attn, p_inv = {}, {}, {}
        for p in pairs:
            dmat, kk, qk = mm[p]
            gamma = jnp.exp(dmat)
            a_mat[p] = strict * (kk * gamma)
            attn[p] = (incl * (qk * gamma)).astype(BF16)
            p_inv[p] = eye - a_mat[p] * msk_ref[2 + n_lv - 1]
        for li in range(n_lv - 2, -1, -1):
            half = {p: _dot(p_inv[p], a_mat[p] * msk_ref[2 + li]) for p in pairs}
            p_inv = {p: p_inv[p] - _dot(half[p], p_inv[p]) for p in pairs}
        sol = {p: _dot(p_inv[p], pre[p]["rhs"]).astype(BF16) for p in pairs}
        for p in pairs:
            nw = _dot_tn(pre[p]["kd"], sol[p])
            ao = _dot(attn[p], sol[p])
            ops[p] = (nw[:, :GDN_DK], nw[:, GDN_DK:].astype(BF16), ao[:, :GDN_DK],
                      (pre[p]["qg"] - ao[:, GDN_DK:]).astype(BF16), pre[p]["g_last"])

    for c in range(n_chunks):
        rows = slice(c * CHUNK, (c + 1) * CHUNK)
        res = {}
        for h in range(GDN_HEADS):
            n_mat, w2, o2, q2, g_last = ops[(c, h)]
            res[h] = jnp.dot(jnp.concatenate([w2, q2], axis=0), st_ref[h].astype(BF16),
                             preferred_element_type=F32)
        for h in range(GDN_HEADS):
            cols = slice(h * GDN_DK, (h + 1) * GDN_DK)
            n_mat, w2, o2, q2, g_last = ops[(c, h)]
            st_ref[h] = st_ref[h] * g_last - res[h][:GDN_DK] + n_mat
            o = res[h][GDN_DK:] + o2
            on = o * lax.rsqrt(jnp.mean(o * o, axis=-1, keepdims=True) + RMS_EPS) * ng
            o_ref[rows, cols] = (on * _silu(z_ref[rows, cols].astype(F32))).astype(BF16)


def _gdn(main, small, conv_w, alog_vec, dtb_vec, norm_g, b, s, ts=512, group=4):
    t = main.shape[0]
    nst = s // ts
    row = lambda bi, si: bi * nst + si
    msk = jnp.asarray(_np_masks(), F32)

    def col(cb):
        return pl.BlockSpec((ts, HEAD_W), lambda bi, si: (row(bi, si), cb))

    const2 = lambda bi, si: (0, 0)
    return pl.pallas_call(
        functools.partial(_gdn_kernel, n_chunks=ts // CHUNK, group=group),
        grid=(b, nst),
        in_specs=[
            col(CB_GDN_Q), col(CB_GDN_K), col(CB_GDN_V), col(CB_GDN_Z),
            pl.BlockSpec((ts, LANES), lambda bi, si: (row(bi, si), 0)),
            pl.BlockSpec(conv_w.shape, const2),
            pl.BlockSpec(alog_vec.shape, const2),
            pl.BlockSpec(dtb_vec.shape, const2),
            pl.BlockSpec(norm_g.shape, const2),
            pl.BlockSpec(msk.shape, lambda bi, si: (0, 0, 0)),
        ],
        out_specs=pl.BlockSpec((ts, HEAD_W), lambda bi, si: (row(bi, si), 0)),
        out_shape=jax.ShapeDtypeStruct((t, HEAD_W), BF16),
        scratch_shapes=[pltpu.VMEM((GDN_HEADS, GDN_DK, GDN_DK), F32),
                        pltpu.VMEM((ts + 8, 3 * HEAD_W), F32),
                        pltpu.VMEM((ts, 3 * HEAD_W), F32)],
        compiler_params=_cparams(("parallel", "arbitrary")),
        name="gdn",
    )(main, main, main, main, small, conv_w, alog_vec, dtb_vec, norm_g, msk)


def _merge_kernel(x_ref, m0_ref, m1_ref, m2_ref, og_ref, oa_ref, od_ref, wg_ref, wa_ref, wd_ref,
                  wo_ref, g2_ref, x1_ref, h2_ref):
    y = (_sigmoid(m0_ref[...].astype(F32)) * jnp.dot(og_ref[...], wg_ref[...], preferred_element_type=F32)
         + _sigmoid(m1_ref[...].astype(F32)) * jnp.dot(oa_ref[...], wa_ref[...], preferred_element_type=F32)
         + _sigmoid(m2_ref[...].astype(F32)) * jnp.dot(od_ref[...], wd_ref[...], preferred_element_type=F32))
    x1 = x_ref[...] + jnp.dot(y.astype(BF16), wo_ref[...], preferred_element_type=F32)
    x1_ref[...] = x1
    h2 = x1 * lax.rsqrt(jnp.mean(x1 * x1, axis=-1, keepdims=True) + RMS_EPS) * g2_ref[...]
    h2_ref[...] = h2.astype(BF16)


def _merge(x, main, o_gla, o_att, o_gdn, wg, wa, wd, wo, g2, tm=512):
    t, d = x.shape
    rowb = lambda i: (i, 0)
    const = lambda i: (0, 0)
    return pl.pallas_call(
        _merge_kernel,
        grid=(t // tm,),
        in_specs=[
            pl.BlockSpec((tm, d), rowb),
            pl.BlockSpec((tm, d), lambda i: (i, 0)),
            pl.BlockSpec((tm, d), lambda i: (i, 1)),
            pl.BlockSpec((tm, d), lambda i: (i, 2)),
            pl.BlockSpec((tm, HEAD_W), rowb),
            pl.BlockSpec((tm, HEAD_W), rowb),
            pl.BlockSpec((tm, HEAD_W), rowb),
            pl.BlockSpec(wg.shape, const),
            pl.BlockSpec(wa.shape, const),
            pl.BlockSpec(wd.shape, const),
            pl.BlockSpec(wo.shape, const),
            pl.BlockSpec(g2.shape, const),
        ],
        out_specs=[pl.BlockSpec((tm, d), rowb), pl.BlockSpec((tm, d), rowb)],
        out_shape=[jax.ShapeDtypeStruct((t, d), F32), jax.ShapeDtypeStruct((t, d), BF16)],
        compiler_params=_cparams(("parallel",)),
        name="merge_out",
    )(x, main, main, main, o_gla, o_att, o_gdn, wg, wa, wd, wo, g2)


def _ffn_kernel(x1_ref, h2_ref, wgu_ref, wdown_ref, gf_ref, o_ref, acc_ref, *, final_norm):
    j = pl.program_id(1)

    @pl.when(j == 0)
    def _():
        acc_ref[...] = x1_ref[...]

    th = wdown_ref.shape[0]
    gu = jnp.dot(h2_ref[...], wgu_ref[...], preferred_element_type=F32)
    acc_ref[...] += jnp.dot((_silu(gu[:, :th]) * gu[:, th:]).astype(BF16), wdown_ref[...],
                            preferred_element_type=F32)

    @pl.when(j == pl.num_programs(1) - 1)
    def _():
        x2 = acc_ref[...]
        if final_norm:
            x2 = x2 * lax.rsqrt(jnp.mean(x2 * x2, axis=-1, keepdims=True) + RMS_EPS) * gf_ref[...]
        o_ref[...] = x2


def _ffn(x1, h2, w_in, w_out, gf, final_norm, tm=512, n_h=2):
    t, d = x1.shape
    hidden = w_out.shape[0]
    th = hidden // n_h
    w_gu = jnp.concatenate([w_in[:, c:c + th] for j in range(n_h) for c in (j * th, hidden + j * th)], axis=1)
    return pl.pallas_call(
        functools.partial(_ffn_kernel, final_norm=final_norm),
        grid=(t // tm, n_h),
        in_specs=[
            pl.BlockSpec((tm, d), lambda i, j: (i, 0)),
            pl.BlockSpec((tm, d), lambda i, j: (i, 0)),
            pl.BlockSpec((d, 2 * th), lambda i, j: (0, j)),
            pl.BlockSpec((th, d), lambda i, j: (j, 0)),
            pl.BlockSpec(gf.shape, lambda i, j: (0, 0)),
        ],
        out_specs=pl.BlockSpec((tm, d), lambda i, j: (i, 0)),
        out_shape=jax.ShapeDtypeStruct((t, d), F32),
        scratch_shapes=[pltpu.VMEM((tm, d), F32)],
        compiler_params=_cparams(("parallel", "arbitrary")),
        name="ffn",
    )(x1, h2, w_gu, w_out, gf)


def _split_w_in(w_in):
    widths = (512, 512, 512, 512, GLA_GATE_RANK, 512, 512, 512, 512, 512, 512, 512,
              GDN_HEADS, GDN_HEADS, w_in.shape[1] - (11 * 512 + GLA_GATE_RANK + 2 * GDN_HEADS))
    bounds = np.cumsum(widths)[:-1]
    (gq, gk, gv, gr, glr, aq, ak, av, dq, dk, dv, dz, da, db, mg) = jnp.split(w_in, bounds, axis=1)
    main = jnp.concatenate([mg, gq, gk, gv, gr, aq, ak, av, dq, dk, dv, dz], axis=1)
    small = jnp.concatenate([glr, da, db], axis=1)
    small = jnp.pad(small, ((0, 0), (0, LANES - small.shape[1])))
    return main.astype(BF16), small.astype(BF16)


def _band_bias(rel_bias):
    left = ATT_LEFT_CHUNKS * CHUNK
    band = left + CHUNK
    n = band + CHUNK - 1
    ext = jnp.concatenate([rel_bias, jnp.repeat(rel_bias[:, -1:], n - rel_bias.shape[1], axis=1)], axis=1)
    r = jnp.roll(ext[:, ::-1], -(CHUNK - 1), axis=1)
    rows = jnp.tile(r, (1, CHUNK))[:, :CHUNK * (n - 1)].reshape(rel_bias.shape[0], CHUNK, n - 1)
    return rows[:, :, :band].astype(F32)


def _lane_vec(vals, offset):
    out = jnp.zeros((1, LANES), F32)
    return lax.dynamic_update_slice(out, vals.reshape(1, -1).astype(F32), (0, offset))


def kernel(x, mix_norm_g, w_in, gla_w_gate_up, gla_b_gate, gla_norm_g, att_rel_bias, gdn_conv_w,
           gdn_a_log, gdn_dt_bias, gdn_norm_g, w_branch_gla, w_branch_att, w_branch_gdn, w_out,
           ffn_norm_g, w_ffn_in, w_ffn_out, final_norm_g):
    b, s, d = x.shape
    depth = w_in.shape[0]
    xt = x.reshape(b * s, d)
    for l in range(depth):
        w_main, w_small = _split_w_in(w_in[l])
        main, small = _inproj(xt, mix_norm_g[l].reshape(1, d), w_main, w_small)
        wgu_pad = jnp.pad(gla_w_gate_up[l], ((0, LANES - GLA_GATE_RANK), (0, 0)))
        o_gla = _gla(main, small, wgu_pad, gla_b_gate[l].reshape(1, -1),
                     gla_norm_g[l].reshape(1, -1), b, s)
        o_att = _att(main, _band_bias(att_rel_bias[l]), b, s)
        o_gdn = _gdn(main, small, gdn_conv_w[l], _lane_vec(gdn_a_log[l], SMALL_A),
                     _lane_vec(gdn_dt_bias[l], SMALL_A), gdn_norm_g[l].reshape(1, -1), b, s)
        x1, h2 = _merge(xt, main, o_gla, o_att, o_gdn, w_branch_gla[l].astype(BF16),
                        w_branch_att[l].astype(BF16), w_branch_gdn[l].astype(BF16),
                        w_out[l].astype(BF16), ffn_norm_g[l].reshape(1, d))
        xt = _ffn(x1, h2, w_ffn_in[l].astype(BF16), w_ffn_out[l].astype(BF16),
                  final_norm_g.reshape(1, d), final_norm=(l == depth - 1))
    return xt.reshape(b, s, d)
```
